```python
import math
import jax, jax.numpy as jnp
from jax import lax
import numpy as np

D_MODEL = 1024
BATCH = 8
SEQ = 2048
DEPTH = 4
DEC_BATCH = 128
DEC_SEQ = 1
PAST_LEN = 8192
PAGE_SIZE = 128

MOBA_HEADS = 8
MOBA_KV_HEADS = 2
MOBA_HEAD_DIM = 64
MOBA_BLOCK = 256
MOBA_TOPK = 3
MOBA_QCHUNK = 16
MLA_HEADS = 8
MLA_Q_RANK = 256
MLA_KV_RANK = 128
MLA_NOPE_DIM = 64
MLA_ROPE_DIM = 32
MLA_V_DIM = 64
ROPE_BASE = 10000.0
MOBA_Q_W = MOBA_HEADS * MOBA_HEAD_DIM
MOBA_KV_W = MOBA_KV_HEADS * MOBA_HEAD_DIM
AB_IN = MOBA_Q_W + 2 * MOBA_KV_W + MLA_Q_RANK + MLA_KV_RANK + MLA_ROPE_DIM
AB_SPLITS = (MOBA_Q_W, MOBA_Q_W + MOBA_KV_W, MOBA_Q_W + 2 * MOBA_KV_W, MOBA_Q_W + 2 * MOBA_KV_W + MLA_Q_RANK, MOBA_Q_W + 2 * MOBA_KV_W + MLA_Q_RANK + MLA_KV_RANK)
AB_OUT = MOBA_HEADS * MOBA_HEAD_DIM + MLA_HEADS * MLA_V_DIM
DIFF_HEADS = 8
DIFF_KV_HEADS = 2
DIFF_HEAD_DIM = 64
C_Q_W = DIFF_HEADS * 2 * DIFF_HEAD_DIM
C_K_W = DIFF_KV_HEADS * 2 * DIFF_HEAD_DIM
C_IN = C_Q_W + 2 * C_K_W
C_OUT = DIFF_HEADS * 2 * DIFF_HEAD_DIM
N_GROUPS = 4
EXPERTS_PER_GROUP = 8
N_EXPERTS = N_GROUPS * EXPERTS_PER_GROUP
TOP_EXPERTS = 2
EXPERT_FF = 256
N_AB_LAYERS = (DEPTH + 1) // 2
N_C_LAYERS = DEPTH // 2
ATTN_QBLOCK = 128
EPS = 1e-6

kernel_name = 'hybrid_moba_mla_diffattn_hmoe_step'


def rms_norm(x, g):
    xf = x.astype(jnp.float32)
    y = xf * lax.rsqrt(jnp.mean(xf * xf, axis=-1, keepdims=True) + EPS)
    return (y * g.astype(jnp.float32)).astype(x.dtype)


def ada_norm(x, g, shift, scale):
    return rms_norm(x, g) * (1 + scale[:, None, :]) + shift[:, None, :]


def modulation(c, w, b):
    m = jax.nn.silu(c) @ w + b
    return jnp.split(m, 6, axis=-1)


def alibi_slopes(n):
    return jnp.exp2(-8.0 * jnp.arange(1, n + 1, dtype=jnp.float32) / n)


def rope(x, pos):
    half = x.shape[-1] // 2
    inv = ROPE_BASE ** (-jnp.arange(half, dtype=jnp.float32) / half)
    ang = pos.astype(jnp.float32)[:, None] * inv[None, :]
    cos = jnp.cos(ang)[None, :, None, :]
    sin = jnp.sin(ang)[None, :, None, :]
    xf = x.astype(jnp.float32)
    x1, x2 = xf[..., :half], xf[..., half:]
    return jnp.concatenate([x1 * cos - x2 * sin, x1 * sin + x2 * cos], axis=-1).astype(x.dtype)


def gather_pages(pool, page_table):
    rows = pool[page_table]
    return rows.reshape((page_table.shape[0], page_table.shape[1] * pool.shape[1]) + pool.shape[2:])


def dense_attention(q, k, v, q_pos, slopes, scale):
    B, T, H, Dk = q.shape
    L, G = k.shape[1], k.shape[2]
    R = H // G
    qb = math.gcd(T, ATTN_QBLOCK)
    nqb = T // qb
    k_pos = jnp.arange(L, dtype=jnp.int32)
    qr = q.reshape(B, nqb, qb, G, R, Dk).transpose(1, 0, 2, 3, 4, 5)
    pr = q_pos.reshape(nqb, qb)

    def block(args):
        qc, pc = args
        s = jnp.einsum('bqgrd,bkgd->bgrqk', qc, k, preferred_element_type=jnp.float32) * scale
        dist = pc[:, None] - k_pos[None, :]
        if slopes is not None:
            s = s - slopes.reshape(G, R)[None, :, :, None, None] * dist.astype(jnp.float32)
        s = jnp.where(dist >= 0, s, -jnp.inf)
        p = jax.nn.softmax(s, axis=-1).astype(v.dtype)
        return jnp.einsum('bgrqk,bkge->bqgre', p, v)

    o = lax.map(block, (qr, pr))
    return o.transpose(1, 0, 2, 3, 4, 5).reshape(B, T, H, v.shape[-1])


def moba_attention(q, k, v, q_pos, slopes):
    B, T, H, D = q.shape
    L, G = k.shape[1], k.shape[2]
    R = H // G
    nb = -(-L // MOBA_BLOCK)
    pad = nb * MOBA_BLOCK - L
    kb = jnp.pad(k, ((0, 0), (0, pad), (0, 0), (0, 0))).reshape(B, nb, MOBA_BLOCK, G, D).transpose(0, 3, 1, 2, 4)
    vb = jnp.pad(v, ((0, 0), (0, pad), (0, 0), (0, 0))).reshape(B, nb, MOBA_BLOCK, G, D).transpose(0, 3, 1, 2, 4)
    k_mean = jnp.mean(kb.astype(jnp.float32), axis=3)
    scale = D ** -0.5
    slopes_gr = slopes.reshape(G, R)
    qc = math.gcd(T, MOBA_QCHUNK)
    nq = T // qc
    qr = q.reshape(B, nq, qc, G, R, D).transpose(1, 0, 2, 3, 4, 5)
    pr = q_pos.reshape(nq, qc)
    blk_ids = jnp.arange(nb, dtype=jnp.int32)
    offs = jnp.arange(MOBA_BLOCK, dtype=jnp.int32)
    b_i = jnp.arange(B)[:, None, None, None, None]
    g_i = jnp.arange(G)[None, :, None, None, None]
    n_flat = MOBA_TOPK * MOBA_BLOCK

    def chunk(args):
        qx, pos = args
        own = pos // MOBA_BLOCK
        n_sel = jnp.minimum(own, MOBA_TOPK)
        gate = jnp.einsum('bqgrd,bgnd->bgrqn', qx.astype(jnp.float32), k_mean)
        gate = jnp.where(blk_ids[None, :] < own[:, None], gate, -jnp.inf)
        if nb < MOBA_TOPK:
            gate = jnp.pad(gate, ((0, 0), (0, 0), (0, 0), (0, 0), (0, MOBA_TOPK - nb)), constant_values=-jnp.inf)
        _, sel = lax.top_k(gate, MOBA_TOPK)
        sel = jnp.minimum(sel, nb - 1)
        sel_ok = jnp.arange(MOBA_TOPK)[None, :] < n_sel[:, None]
        ks = kb[b_i, g_i, sel]
        vs = vb[b_i, g_i, sel]
        ko = kb[:, :, own]
        vo = vb[:, :, own]
        s_sel = jnp.einsum('bqgrd,bgrqkjd->bgrqkj', qx, ks, preferred_element_type=jnp.float32) * scale
        dist_sel = pos[None, None, None, :, None, None] - (sel[..., None] * MOBA_BLOCK + offs)
        s_sel = s_sel - slopes_gr[None, :, :, None, None, None] * dist_sel.astype(jnp.float32)
        s_sel = jnp.where(sel_ok[None, None, None, :, :, None], s_sel, -jnp.inf)
        s_own = jnp.einsum('bqgrd,bgqjd->bgrqj', qx, ko, preferred_element_type=jnp.float32) * scale
        dist_own = pos[:, None] - (own[:, None] * MOBA_BLOCK + offs[None, :])
        s_own = s_own - slopes_gr[None, :, :, None, None] * dist_own.astype(jnp.float32)
        s_own = jnp.where(dist_own >= 0, s_own, -jnp.inf)
        s = jnp.concatenate([s_sel.reshape(B, G, R, qc, n_flat), s_own], axis=-1)
        p = jax.nn.softmax(s, axis=-1).astype(v.dtype)
        p_sel = p[..., :n_flat].reshape(B, G, R, qc, MOBA_TOPK, MOBA_BLOCK)
        p_own = p[..., n_flat:]
        return jnp.einsum('bgrqkj,bgrqkjd->bqgrd', p_sel, vs) + jnp.einsum('bgrqj,bgqjd->bqgrd', p_own, vo)

    o = lax.map(chunk, (qr, pr))
    return o.transpose(1, 0, 2, 3, 4, 5).reshape(B, T, H, D)


def ab_mixer(h, q_pos, past, slopes, w_in, q_norm_g, kv_norm_g, w_uq, w_uk, w_uv, w_out):
    B, T, _ = h.shape
    z = h @ w_in
    mq, mk, mv, cq, ckv, kpe = jnp.split(z, AB_SPLITS, axis=-1)
    mq = mq.reshape(B, T, MOBA_HEADS, MOBA_HEAD_DIM)
    mk = mk.reshape(B, T, MOBA_KV_HEADS, MOBA_HEAD_DIM)
    mv = mv.reshape(B, T, MOBA_KV_HEADS, MOBA_HEAD_DIM)
    ckv = rms_norm(ckv, kv_norm_g)
    kpe = rope(kpe[:, :, None, :], q_pos)[:, :, 0, :]
    new = (mk, mv, ckv, kpe)
    if past is None:
        k_f, v_f, ckv_f, kpe_f = new
    else:
        k_f, v_f, ckv_f, kpe_f = [jnp.concatenate([p_, n_], axis=1) for p_, n_ in zip(past, new)]
    o_a = moba_attention(mq, k_f, v_f, q_pos, slopes)
    q = (rms_norm(cq, q_norm_g) @ w_uq).reshape(B, T, MLA_HEADS, MLA_NOPE_DIM + MLA_ROPE_DIM)
    q_nope, q_pe = q[..., :MLA_NOPE_DIM], rope(q[..., MLA_NOPE_DIM:], q_pos)
    q_lat = jnp.einsum('bthn,rhn->bthr', q_nope, w_uk)
    q_full = jnp.concatenate([q_lat, q_pe], axis=-1)
    k_full = jnp.concatenate([ckv_f, kpe_f], axis=-1)[:, :, None, :]
    o_lat = dense_attention(q_full, k_full, ckv_f[:, :, None, :], q_pos, None, (MLA_NOPE_DIM + MLA_ROPE_DIM) ** -0.5)
    o_b = jnp.einsum('bthr,rhv->bthv', o_lat, w_uv)
    out = jnp.concatenate([o_a.reshape(B, T, -1), o_b.reshape(B, T, -1)], axis=-1) @ w_out
    return out, new


def c_mixer(h, q_pos, past, slopes, lam_init, w_in, lam, subln_g, w_out):
    B, T, _ = h.shape
    z = h @ w_in
    q, k, v = jnp.split(z, (C_Q_W, C_Q_W + C_K_W), axis=-1)
    q = q.reshape(B, T, DIFF_HEADS, 2, DIFF_HEAD_DIM)
    k = k.reshape(B, T, DIFF_KV_HEADS, 2, DIFF_HEAD_DIM)
    v = v.reshape(B, T, DIFF_KV_HEADS, 2 * DIFF_HEAD_DIM)
    new = (k, v)
    if past is None:
        k_f, v_f = new
    else:
        k_f, v_f = [jnp.concatenate([p_, n_], axis=1) for p_, n_ in zip(past, new)]
    lf = lam.astype(jnp.float32)
    lam_full = jnp.exp(jnp.sum(lf[0] * lf[1])) - jnp.exp(jnp.sum(lf[2] * lf[3])) + lam_init
    scale = DIFF_HEAD_DIM ** -0.5
    o1 = dense_attention(q[:, :, :, 0], k_f[:, :, :, 0], v_f, q_pos, slopes, scale)
    o2 = dense_attention(q[:, :, :, 1], k_f[:, :, :, 1], v_f, q_pos, slopes, scale)
    o = o1 - lam_full.astype(o1.dtype) * o2
    o = rms_norm(o, subln_g) * (1 - lam_init)
    return o.reshape(B, T, C_OUT) @ w_out, new


def hier_moe(h, w_group, b_group, w_expert, b_expert, w1, w3, w2):
    B, T, D = h.shape
    t = h.reshape(B * T, D)
    g_prob = jax.nn.softmax((t @ w_group + b_group).astype(jnp.float32), axis=-1)
    g_w, g_idx = lax.top_k(g_prob, 1)
    e_logit = (t @ w_expert + b_expert).astype(jnp.float32).reshape(-1, N_GROUPS, EXPERTS_PER_GROUP)
    e_in = jnp.take_along_axis(e_logit, g_idx[:, :, None], axis=1)[:, 0]
    e_w, e_idx = lax.top_k(jax.nn.softmax(e_in, axis=-1), TOP_EXPERTS)
    e_w = e_w / jnp.sum(e_w, axis=-1, keepdims=True)
    weights = g_w * e_w
    expert_id = g_idx * EXPERTS_PER_GROUP + e_idx
    gates = jnp.sum(jax.nn.one_hot(expert_id, N_EXPERTS, dtype=jnp.float32) * weights[..., None], axis=1)
    hid = jax.nn.silu(jnp.einsum('nd,edf->nef', t, w1)) * jnp.einsum('nd,edf->nef', t, w3)
    hid = hid * gates[:, :, None].astype(hid.dtype)
    return jnp.einsum('nef,efd->nd', hid, w2).reshape(B, T, D)


def setup_inputs(seed: int = 0) -> dict:
    key = jax.random.key(seed)
    ks = iter(jax.random.split(key, 48))

    def nrm(shape, s):
        return jax.random.normal(next(ks), shape, jnp.float32) * s

    def gain(shape):
        return 1.0 + nrm(shape, 0.01)

    n_pages = PAST_LEN // PAGE_SIZE
    n_used = DEC_BATCH * n_pages
    n_pool = n_used + n_used // 4
    d = D_MODEL
    inp = {}
    inp['x_prompt'] = nrm((BATCH, SEQ, d), 1.0)
    inp['x_sample'] = nrm((DEC_BATCH, DEC_SEQ, d), 1.0)
    inp['c_prompt'] = nrm((BATCH, d), 1.0)
    inp['c_sample'] = nrm((DEC_BATCH, d), 1.0)
    inp['cache_moba_k'] = nrm((N_AB_LAYERS, n_pool, PAGE_SIZE, MOBA_KV_HEADS, MOBA_HEAD_DIM), 1.0)
    inp['cache_moba_v'] = nrm((N_AB_LAYERS, n_pool, PAGE_SIZE, MOBA_KV_HEADS, MOBA_HEAD_DIM), 1.0)
    inp['cache_mla_ckv'] = nrm((N_AB_LAYERS, n_pool, PAGE_SIZE, MLA_KV_RANK), 1.0)
    inp['cache_mla_kpe'] = nrm((N_AB_LAYERS, n_pool, PAGE_SIZE, MLA_ROPE_DIM), 1.0)
    inp['cache_diff_k'] = nrm((N_C_LAYERS, n_pool, PAGE_SIZE, DIFF_KV_HEADS, 2, DIFF_HEAD_DIM), 1.0)
    inp['cache_diff_v'] = nrm((N_C_LAYERS, n_pool, PAGE_SIZE, DIFF_KV_HEADS, 2 * DIFF_HEAD_DIM), 1.0)
    inp['page_table'] = jax.random.permutation(next(ks), n_pool)[:n_used].reshape(DEC_BATCH, n_pages).astype(jnp.int32)
    inp['mod_w'] = nrm((DEPTH, d, 6 * d), 0.5 * d ** -0.5)
    inp['mod_b'] = nrm((DEPTH, 6 * d), 0.01)
    inp['norm_mix'] = gain((DEPTH, d))
    inp['norm_ffn'] = gain((DEPTH, d))
    inp['ab_w_in'] = nrm((N_AB_LAYERS, d, AB_IN), d ** -0.5)
    inp['mla_q_norm'] = gain((N_AB_LAYERS, MLA_Q_RANK))
    inp['mla_kv_norm'] = gain((N_AB_LAYERS, MLA_KV_RANK))
    inp['mla_w_uq'] = nrm((N_AB_LAYERS, MLA_Q_RANK, MLA_HEADS * (MLA_NOPE_DIM + MLA_ROPE_DIM)), MLA_Q_RANK ** -0.5)
    inp['mla_w_uk'] = nrm((N_AB_LAYERS, MLA_KV_RANK, MLA_HEADS, MLA_NOPE_DIM), MLA_KV_RANK ** -0.5)
    inp['mla_w_uv'] = nrm((N_AB_LAYERS, MLA_KV_RANK, MLA_HEADS, MLA_V_DIM), MLA_KV_RANK ** -0.5)
    inp['ab_w_out'] = nrm((N_AB_LAYERS, AB_OUT, d), AB_OUT ** -0.5)
    inp['c_w_in'] = nrm((N_C_LAYERS, d, C_IN), d ** -0.5)
    inp['diff_lambda'] = nrm((N_C_LAYERS, 4, DIFF_HEAD_DIM), 0.1)
    inp['diff_subln'] = gain((N_C_LAYERS, 2 * DIFF_HEAD_DIM))
    inp['c_w_out'] = nrm((N_C_LAYERS, C_OUT, d), C_OUT ** -0.5)
    inp['moe_w_group'] = nrm((DEPTH, d, N_GROUPS), d ** -0.5)
    inp['moe_b_group'] = nrm((DEPTH, N_GROUPS), 0.01)
    inp['moe_w_expert'] = nrm((DEPTH, d, N_EXPERTS), d ** -0.5)
    inp['moe_b_expert'] = nrm((DEPTH, N_EXPERTS), 0.01)
    inp['moe_w1'] = nrm((DEPTH, N_EXPERTS, d, EXPERT_FF), d ** -0.5)
    inp['moe_w3'] = nrm((DEPTH, N_EXPERTS, d, EXPERT_FF), d ** -0.5)
    inp['moe_w2'] = nrm((DEPTH, N_EXPERTS, EXPERT_FF, d), EXPERT_FF ** -0.5)
    inp['final_norm'] = gain((d,))
    return inp


def reference(x_prompt, x_sample, c_prompt, c_sample, cache_moba_k, cache_moba_v, cache_mla_ckv, cache_mla_kpe, cache_diff_k, cache_diff_v, page_table, mod_w, mod_b, norm_mix, norm_ffn, ab_w_in, mla_q_norm, mla_kv_norm, mla_w_uq, mla_w_uk, mla_w_uv, ab_w_out, c_w_in, diff_lambda, diff_subln, c_w_out, moe_w_group, moe_b_group, moe_w_expert, moe_b_expert, moe_w1, moe_w3, moe_w2, final_norm):
    t_p, t_s = x_prompt.shape[1], x_sample.shape[1]
    past_len = page_table.shape[1] * cache_moba_k.shape[2]
    pos_p = jnp.arange(t_p, dtype=jnp.int32)
    pos_s = past_len + jnp.arange(t_s, dtype=jnp.int32)
    slopes_moba = alibi_slopes(MOBA_HEADS)
    slopes_diff = alibi_slopes(DIFF_HEADS)
    xp, xs = x_prompt, x_sample
    mk_p, mv_p, ckv_p, kpe_p, dk_p, dv_p = [], [], [], [], [], []
    mk_s, mv_s, ckv_s, kpe_s, dk_s, dv_s = [], [], [], [], [], []
    for l in range(DEPTH):
        mp = modulation(c_prompt, mod_w[l], mod_b[l])
        ms = modulation(c_sample, mod_w[l], mod_b[l])
        hp = ada_norm(xp, norm_mix[l], mp[0], mp[1])
        hs = ada_norm(xs, norm_mix[l], ms[0], ms[1])
        i = l // 2
        if l % 2 == 0:
            wts = (ab_w_in[i], mla_q_norm[i], mla_kv_norm[i], mla_w_uq[i], mla_w_uk[i], mla_w_uv[i], ab_w_out[i])
            op, newp = ab_mixer(hp, pos_p, None, slopes_moba, *wts)
            past = (gather_pages(cache_moba_k[i], page_table), gather_pages(cache_moba_v[i], page_table), gather_pages(cache_mla_ckv[i], page_table), gather_pages(cache_mla_kpe[i], page_table))
            os_, news = ab_mixer(hs, pos_s, past, slopes_moba, *wts)
            mk_p.append(newp[0]); mv_p.append(newp[1]); ckv_p.append(newp[2]); kpe_p.append(newp[3])
            mk_s.append(news[0]); mv_s.append(news[1]); ckv_s.append(news[2]); kpe_s.append(news[3])
        else:
            lam_init = 0.8 - 0.6 * math.exp(-0.3 * l)
            wts = (c_w_in[i], diff_lambda[i], diff_subln[i], c_w_out[i])
            op, newp = c_mixer(hp, pos_p, None, slopes_diff, lam_init, *wts)
            past = (gather_pages(cache_diff_k[i], page_table), gather_pages(cache_diff_v[i], page_table))
            os_, news = c_mixer(hs, pos_s, past, slopes_diff, lam_init, *wts)
            dk_p.append(newp[0]); dv_p.append(newp[1])
            dk_s.append(news[0]); dv_s.append(news[1])
        xp = xp + mp[2][:, None, :] * op
        xs = xs + ms[2][:, None, :] * os_
        moe_w = (moe_w_group[l], moe_b_group[l], moe_w_expert[l], moe_b_expert[l], moe_w1[l], moe_w3[l], moe_w2[l])
        xp = xp + mp[5][:, None, :] * hier_moe(ada_norm(xp, norm_ffn[l], mp[3], mp[4]), *moe_w)
        xs = xs + ms[5][:, None, :] * hier_moe(ada_norm(xs, norm_ffn[l], ms[3], ms[4]), *moe_w)
    y_prompt = rms_norm(xp, final_norm)
    y_sample = rms_norm(xs, final_norm)
    return (y_prompt, y_sample, jnp.stack(mk_p), jnp.stack(mv_p), jnp.stack(ckv_p), jnp.stack(kpe_p), jnp.stack(dk_p), jnp.stack(dv_p), jnp.stack(mk_s), jnp.stack(mv_s), jnp.stack(ckv_s), jnp.stack(kpe_s), jnp.stack(dk_s), jnp.stack(dv_s))
```

```python
import functools
import math

import jax
import jax.numpy as jnp
from jax import lax
from jax.experimental import pallas as pl
from jax.experimental.pallas import tpu as pltpu

F32 = jnp.float32
BF16 = jnp.bfloat16
HIGHEST = lax.Precision.HIGHEST

D_MODEL = 1024
MOBA_HEADS, MOBA_KV_HEADS, MOBA_HEAD_DIM, MOBA_BLOCK, MOBA_TOPK = 8, 2, 64, 256, 3
MLA_HEADS, MLA_Q_RANK, MLA_KV_RANK, MLA_NOPE_DIM, MLA_ROPE_DIM, MLA_V_DIM = 8, 256, 128, 64, 32, 64
ROPE_BASE = 10000.0
DIFF_HEADS, DIFF_KV_HEADS, DIFF_HEAD_DIM = 8, 2, 64
N_GROUPS, EXPERTS_PER_GROUP, EXPERT_FF = 4, 8, 256
N_EXPERTS = N_GROUPS * EXPERTS_PER_GROUP
EPS = 1e-6
NEG = -1e30

LANE = 128
VMEM_LIMIT = 56 * 1024 * 1024
TOKEN_TILE = 256
MOE_TILE = 1024
AB_IN_EXT = 512 + 128 + 128 + 256 + 128 + 128 + 128
UQ_EXT = 512 + 1024 + 1024


def _cparams(sem):
    return pltpu.CompilerParams(dimension_semantics=sem, vmem_limit_bytes=VMEM_LIMIT)


def _alibi_slopes(n):
    return [2.0 ** (-8.0 * (i + 1) / n) for i in range(n)]


def _slope_column(values):
    row = lax.broadcasted_iota(jnp.int32, (len(values), 1), 0)
    col = jnp.zeros((len(values), 1), F32)
    for r, v in enumerate(values):
        col = jnp.where(row == r, v, col)
    return col


def _rms(x, g):
    return x * lax.rsqrt(jnp.mean(x * x, axis=-1, keepdims=True) + EPS) * g


def _ada(x, g, shift, scale):
    return _rms(x, g) * (1.0 + scale) + shift


def _mod_kernel(c_ref, w_ref, b_ref, o_ref):
    c = c_ref[...]
    a = c / (1.0 + jnp.exp(-c))
    o_ref[0] = jnp.dot(a, w_ref[0], preferred_element_type=F32, precision=HIGHEST) + b_ref[0]


def _modulation(c_all, mod_w, mod_b):
    depth, d, n6 = mod_w.shape
    rows = c_all.shape[0]
    tn = 512
    return pl.pallas_call(
        _mod_kernel,
        grid=(depth, n6 // tn),
        in_specs=[
            pl.BlockSpec((rows, d), lambda l, j: (0, 0)),
            pl.BlockSpec((1, d, tn), lambda l, j: (l, 0, j)),
            pl.BlockSpec((1, 1, tn), lambda l, j: (l, 0, j)),
        ],
        out_specs=pl.BlockSpec((1, rows, tn), lambda l, j: (l, 0, j)),
        out_shape=jax.ShapeDtypeStruct((depth, rows, n6), F32),
        compiler_params=_cparams(("arbitrary", "arbitrary")),
        name="modulation",
    )(c_all, mod_w, mod_b.reshape(depth, 1, n6))


def _mod_spec(per_row, tm, tiles_per_batch):
    if per_row:
        return pl.BlockSpec((tm, D_MODEL), lambda t: (t, 0))
    return pl.BlockSpec((None, 1, D_MODEL), lambda t: (t // tiles_per_batch, 0, 0))


def _row_spec(per_row, tm, width, tiles_per_batch):
    if per_row:
        return pl.BlockSpec((1, width), lambda t: (0, 0))
    return pl.BlockSpec((tm, width), lambda t: (t % tiles_per_batch, 0))


def _full_spec(shape):
    nd = len(shape)
    return pl.BlockSpec(shape, lambda t: (0,) * nd)


def _inproj_ab_kernel(x_ref, sh_ref, sc_ref, g_ref, w1_ref, qg_ref, kvg_ref, wuq_ref, wukt_ref, cos_ref, sin_ref,
                      qm_ref, mk_ref, mv_ref, kvb_ref, kmean_ref, qfull_ref, ckv_ref, kpe_ref, kfull_ref):
    h = _ada(x_ref[...], g_ref[...], sh_ref[...], sc_ref[...]).astype(BF16)
    z = jnp.dot(h, w1_ref[...], preferred_element_type=F32)
    qm_ref[...] = z[:, 0:512].astype(BF16)
    mk = z[:, 512:640]
    mv = z[:, 640:768]
    mk_ref[...] = mk
    mv_ref[...] = mv
    kvb_ref[:, 0:128] = mk.astype(BF16)
    kvb_ref[:, 128:256] = mv.astype(BF16)
    kmean_ref[0] = jnp.mean(mk, axis=0, keepdims=True)
    cos = cos_ref[...]
    sin = sin_ref[...]
    ckv = _rms(z[:, 1024:1152], kvg_ref[...])
    ckv_ref[...] = ckv
    kpe = z[:, 1152:1280] * cos + z[:, 1280:1408] * sin
    kpe_ref[...] = kpe[:, 0:MLA_ROPE_DIM]
    kfull_ref[:, 0:128] = ckv.astype(BF16)
    kfull_ref[:, 128:256] = kpe.astype(BF16)
    cqn = _rms(z[:, 768:1024], qg_ref[...]).astype(BF16)
    q2 = jnp.dot(cqn, wuq_ref[...], preferred_element_type=F32)
    for hd in range(MLA_HEADS):
        nope = q2[:, hd * 64:(hd + 1) * 64].astype(BF16)
        qfull_ref[:, hd * 256:hd * 256 + 128] = jnp.dot(nope, wukt_ref[hd], preferred_element_type=F32).astype(BF16)
        pe = q2[:, 512 + hd * 128:512 + (hd + 1) * 128] * cos + q2[:, 1536 + hd * 128:1536 + (hd + 1) * 128] * sin
        qfull_ref[:, hd * 256 + 128:(hd + 1) * 256] = pe.astype(BF16)


def _inproj_ab(x, shift, scale, g, w1, qg, kvg, wuq, wukt, cos_t, sin_t, per_row, tm, tiles_per_batch):
    n = x.shape[0]
    nt = n // tm
    row = lambda w: pl.BlockSpec((tm, w), lambda t: (t, 0))
    outs = [
        (jax.ShapeDtypeStruct((n, 512), BF16), row(512)),
        (jax.ShapeDtypeStruct((n, 128), F32), row(128)),
        (jax.ShapeDtypeStruct((n, 128), F32), row(128)),
        (jax.ShapeDtypeStruct((n, 256), BF16), row(256)),
        (jax.ShapeDtypeStruct((nt, 1, 128), F32), pl.BlockSpec((1, 1, 128), lambda t: (t, 0, 0))),
        (jax.ShapeDtypeStruct((n, 2048), BF16), row(2048)),
        (jax.ShapeDtypeStruct((n, 128), F32), row(128)),
        (jax.ShapeDtypeStruct((n, MLA_ROPE_DIM), F32), row(MLA_ROPE_DIM)),
        (jax.ShapeDtypeStruct((n, 256), BF16), row(256)),
    ]
    return pl.pallas_call(
        _inproj_ab_kernel,
        grid=(nt,),
        in_specs=[
            row(D_MODEL), _mod_spec(per_row, tm, tiles_per_batch), _mod_spec(per_row, tm, tiles_per_batch),
            _full_spec((1, D_MODEL)), _full_spec((D_MODEL, AB_IN_EXT)), _full_spec((1, MLA_Q_RANK)),
            _full_spec((1, MLA_KV_RANK)), _full_spec((MLA_Q_RANK, UQ_EXT)), _full_spec((MLA_HEADS, 64, 128)),
            _row_spec(per_row, tm, LANE, tiles_per_batch), _row_spec(per_row, tm, LANE, tiles_per_batch),
        ],
        out_specs=[o[1] for o in outs],
        out_shape=[o[0] for o in outs],
        compiler_params=_cparams(("arbitrary",)),
        name="inproj_ab",
    )(x, shift, scale, g, w1, qg, kvg, wuq, wukt, cos_t, sin_t)


def _inproj_c_kernel(x_ref, sh_ref, sc_ref, g_ref, w_ref, q_ref, k_ref, v_ref, kb_ref, vb_ref):
    h = _ada(x_ref[...], g_ref[...], sh_ref[...], sc_ref[...]).astype(BF16)
    z = jnp.dot(h, w_ref[...], preferred_element_type=F32)
    q_ref[...] = z[:, 0:1024].astype(BF16)
    k = z[:, 1024:1280]
    v = z[:, 1280:1536]
    k_ref[...] = k
    v_ref[...] = v
    kb_ref[...] = k.astype(BF16)
    vb_ref[...] = v.astype(BF16)


def _inproj_c(x, shift, scale, g, w, per_row, tm, tiles_per_batch):
    n = x.shape[0]
    row = lambda wd: pl.BlockSpec((tm, wd), lambda t: (t, 0))
    return pl.pallas_call(
        _inproj_c_kernel,
        grid=(n // tm,),
        in_specs=[row(D_MODEL), _mod_spec(per_row, tm, tiles_per_batch), _mod_spec(per_row, tm, tiles_per_batch),
                  _full_spec((1, D_MODEL)), _full_spec((D_MODEL, 1536))],
        out_specs=[row(1024), row(256), row(256), row(256), row(256)],
        out_shape=[jax.ShapeDtypeStruct((n, 1024), BF16), jax.ShapeDtypeStruct((n, 256), F32),
                   jax.ShapeDtypeStruct((n, 256), F32), jax.ShapeDtypeStruct((n, 256), BF16),
                   jax.ShapeDtypeStruct((n, 256), BF16)],
        compiler_params=_cparams(("arbitrary",)),
        name="inproj_c",
    )(x, shift, scale, g, w)


def _outproj_ab_kernel(x_ref, gate_ref, oa_ref, olat_ref, wuv_ref, wout_ref, o_ref):
    ob = jnp.dot(olat_ref[...], wuv_ref[...], preferred_element_type=F32).astype(BF16)
    y = jnp.dot(oa_ref[...], wout_ref[0:512, :], preferred_element_type=F32)
    y = y + jnp.dot(ob, wout_ref[512:1024, :], preferred_element_type=F32)
    o_ref[...] = x_ref[...] + gate_ref[...] * y


def _outproj_ab(x, gate, oa, olat, wuv_bd, wout, per_row, tm, tiles_per_batch):
    n = x.shape[0]
    row = lambda wd: pl.BlockSpec((tm, wd), lambda t: (t, 0))
    return pl.pallas_call(
        _outproj_ab_kernel,
        grid=(n // tm,),
        in_specs=[row(D_MODEL), _mod_spec(per_row, tm, tiles_per_batch), row(512), row(1024),
                  _full_spec((1024, 512)), _full_spec((1024, D_MODEL))],
        out_specs=row(D_MODEL),
        out_shape=jax.ShapeDtypeStruct((n, D_MODEL), F32),
        compiler_params=_cparams(("arbitrary",)),
        name="outproj_ab",
    )(x, gate, oa, olat, wuv_bd, wout)


def _outproj_c_kernel(x_ref, gate_ref, o_in_ref, wout_ref, o_ref):
    y = jnp.dot(o_in_ref[...], wout_ref[...], preferred_element_type=F32)
    o_ref[...] = x_ref[...] + gate_ref[...] * y


def _outproj_c(x, gate, o_in, wout, per_row, tm, tiles_per_batch):
    n = x.shape[0]
    row = lambda wd: pl.BlockSpec((tm, wd), lambda t: (t, 0))
    return pl.pallas_call(
        _outproj_c_kernel,
        grid=(n // tm,),
        in_specs=[row(D_MODEL), _mod_spec(per_row, tm, tiles_per_batch), row(1024), _full_spec((1024, D_MODEL))],
        out_specs=row(D_MODEL),
        out_shape=jax.ShapeDtypeStruct((n, D_MODEL), F32),
        compiler_params=_cparams(("arbitrary",)),
        name="outproj_c",
    )(x, gate, o_in, wout)


def _router_kernel(x_ref, sh_ref, sc_ref, g_ref, wr_ref, br_ref, h_ref, gates_ref):
    h = _ada(x_ref[...], g_ref[...], sh_ref[...], sc_ref[...])
    h_ref[...] = h.astype(BF16)
    logit = jnp.dot(h, wr_ref[...], preferred_element_type=F32, precision=HIGHEST) + br_ref[...]
    lane = lax.broadcasted_iota(jnp.int32, logit.shape, 1)
    is_grp = (lane >= N_EXPERTS) & (lane < N_EXPERTS + N_GROUPS)
    lg = jnp.where(is_grp, logit, -jnp.inf)
    mg = jnp.max(lg, axis=-1, keepdims=True)
    g_w = 1.0 / jnp.sum(jnp.exp(lg - mg), axis=-1, keepdims=True)
    g_idx = jnp.min(jnp.where(lg == mg, lane, 4 * LANE), axis=-1, keepdims=True) - N_EXPERTS
    in_grp = (lane < N_EXPERTS) & ((lane // EXPERTS_PER_GROUP) == g_idx)
    le = jnp.where(in_grp, logit, -jnp.inf)
    m1 = jnp.max(le, axis=-1, keepdims=True)
    i1 = jnp.min(jnp.where(le == m1, lane, 4 * LANE), axis=-1, keepdims=True)
    le2 = jnp.where(lane == i1, -jnp.inf, le)
    m2 = jnp.max(le2, axis=-1, keepdims=True)
    i2 = jnp.min(jnp.where(le2 == m2, lane, 4 * LANE), axis=-1, keepdims=True)
    r = jnp.exp(m2 - m1)
    w1 = g_w / (1.0 + r)
    w2 = g_w * r / (1.0 + r)
    gates_ref[...] = jnp.where(lane == i1, w1, 0.0) + jnp.where(lane == i2, w2, 0.0)


def _router(x, shift, scale, g, wr, br, per_row, tm, tiles_per_batch):
    n = x.shape[0]
    row = lambda wd: pl.BlockSpec((tm, wd), lambda t: (t, 0))
    return pl.pallas_call(
        _router_kernel,
        grid=(n // tm,),
        in_specs=[row(D_MODEL), _mod_spec(per_row, tm, tiles_per_batch), _mod_spec(per_row, tm, tiles_per_batch),
                  _full_spec((1, D_MODEL)), _full_spec((D_MODEL, LANE)), _full_spec((1, LANE))],
        out_specs=[row(D_MODEL), row(LANE)],
        out_shape=[jax.ShapeDtypeStruct((n, D_MODEL), BF16), jax.ShapeDtypeStruct((n, LANE), F32)],
        compiler_params=_cparams(("arbitrary",)),
        name="router",
    )(x, shift, scale, g, wr, br)


def _moe_kernel(h_ref, gates_ref, w1_ref, w3_ref, w2_ref, x_ref, gmod_ref, o_ref, acc_ref):
    e = pl.program_id(1)

    @pl.when(e == 0)
    def _():
        acc_ref[...] = jnp.zeros_like(acc_ref)

    h = h_ref[...]
    a = jnp.dot(h, w1_ref[0], preferred_element_type=F32)
    b = jnp.dot(h, w3_ref[0], preferred_element_type=F32)
    gates = gates_ref[...]
    lane = lax.broadcasted_iota(jnp.int32, gates.shape, 1)
    gcol = jnp.sum(jnp.where(lane == e, gates, 0.0), axis=-1, keepdims=True)
    hid = (a / (1.0 + jnp.exp(-a))) * b * gcol
    acc_ref[...] += jnp.dot(hid.astype(BF16), w2_ref[0], preferred_element_type=F32)

    @pl.when(e == N_EXPERTS - 1)
    def _():
        o_ref[...] = x_ref[...] + gmod_ref[...] * acc_ref[...]


def _moe(h, gates, w1, w3, w2, x, gmod, per_row, tm, tiles_per_batch):
    n = x.shape[0]
    row = lambda wd: pl.BlockSpec((tm, wd), lambda t, e: (t, 0))
    if per_row:
        gspec = pl.BlockSpec((tm, D_MODEL), lambda t, e: (t, 0))
    else:
        gspec = pl.BlockSpec((None, 1, D_MODEL), lambda t, e: (t // tiles_per_batch, 0, 0))
    return pl.pallas_call(
        _moe_kernel,
        grid=(n // tm, N_EXPERTS),
        in_specs=[row(D_MODEL), row(LANE),
                  pl.BlockSpec((1, D_MODEL, EXPERT_FF), lambda t, e: (e, 0, 0)),
                  pl.BlockSpec((1, D_MODEL, EXPERT_FF), lambda t, e: (e, 0, 0)),
                  pl.BlockSpec((1, EXPERT_FF, D_MODEL), lambda t, e: (e, 0, 0)),
                  row(D_MODEL), gspec],
        out_specs=row(D_MODEL),
        out_shape=jax.ShapeDtypeStruct((n, D_MODEL), F32),
        scratch_shapes=[pltpu.VMEM((tm, D_MODEL), F32)],
        compiler_params=_cparams(("arbitrary", "arbitrary")),
        name="moe_dense",
    )(h, gates, w1, w3, w2, x, gmod)


def _final_norm_kernel(x_ref, g_ref, o_ref):
    o_ref[...] = _rms(x_ref[...], g_ref[...])


def _final_norm(x, g, tm):
    n = x.shape[0]
    return pl.pallas_call(
        _final_norm_kernel,
        grid=(n // tm,),
        in_specs=[pl.BlockSpec((tm, D_MODEL), lambda t: (t, 0)), _full_spec((1, D_MODEL))],
        out_specs=pl.BlockSpec((tm, D_MODEL), lambda t: (t, 0)),
        out_shape=jax.ShapeDtypeStruct((n, D_MODEL), F32),
        compiler_params=_cparams(("arbitrary",)),
        name="final_norm",
    )(x, g)


def _flash_update(carry, s, v):
    m, l, acc = carry
    m_new = jnp.maximum(m, jnp.max(s, axis=-1, keepdims=True))
    alpha = jnp.exp(m - m_new)
    p = jnp.exp(s - m_new)
    l = alpha * l + jnp.sum(p, axis=-1, keepdims=True)
    acc = alpha * acc + jnp.dot(p.astype(BF16), v, preferred_element_type=F32)
    return m_new, l, acc


def _flash_init(tq, dv):
    return (jnp.full((tq, 1), NEG, F32), jnp.zeros((tq, 1), F32), jnp.zeros((tq, dv), F32))


def _qk(q, k):
    return lax.dot_general(q, k, (((1,), (1,)), ((), ())), preferred_element_type=F32)


def _dist_tile(tq, tk):
    r = lax.broadcasted_iota(jnp.int32, (tq, tk), 0)
    c = lax.broadcasted_iota(jnp.int32, (tq, tk), 1)
    return r - c


def _moba_prompt_kernel(q_ref, kv_ref, kmean_ref, o_ref, *, nb):
    i = pl.program_id(1)
    tq = MOBA_BLOCK
    scale = MOBA_HEAD_DIM ** -0.5
    slopes = _alibi_slopes(MOBA_HEADS)
    dist0 = _dist_tile(tq, tq)
    dist0f = dist0.astype(F32)
    lane = lax.broadcasted_iota(jnp.int32, (tq, LANE), 1)
    rep = MOBA_HEADS // MOBA_KV_HEADS
    for h in range(MOBA_HEADS):
        g = h // rep
        qh = q_ref[:, h * 64:(h + 1) * 64]
        km = kmean_ref[0][:, g * 64:(g + 1) * 64]
        gate = lax.dot_general(qh.astype(F32), km, (((1,), (1,)), ((), ())), preferred_element_type=F32,
                               precision=HIGHEST)
        gate = jnp.where(lane < i, gate, -jnp.inf)
        rank = jnp.zeros((tq, LANE), jnp.int32)
        for mb in range(nb):
            gm = gate[:, mb:mb + 1]
            beats = (gm > gate) | ((gm == gate) & (lane > mb))
            rank = rank + beats.astype(jnp.int32)
        sel = jnp.where((lane < i) & (rank < MOBA_TOPK), 1.0, 0.0)

        def body(j, carry, qh=qh, g=g, sel=sel, slope=slopes[h]):
            off = pl.multiple_of(j * MOBA_BLOCK, MOBA_BLOCK)
            kj = kv_ref[pl.ds(off, MOBA_BLOCK), g * 64:(g + 1) * 64]
            vj = kv_ref[pl.ds(off, MOBA_BLOCK), 128 + g * 64:128 + (g + 1) * 64]
            s = _qk(qh, kj) * scale
            s = s - slope * (dist0 + (i - j) * MOBA_BLOCK).astype(F32)
            selcol = jnp.sum(jnp.where(lane == j, sel, 0.0), axis=-1, keepdims=True) > 0.5
            s = jnp.where(selcol, s, NEG)
            return _flash_update(carry, s, vj)

        carry = lax.fori_loop(0, i, body, _flash_init(tq, 64))
        off = pl.multiple_of(i * MOBA_BLOCK, MOBA_BLOCK)
        kj = kv_ref[pl.ds(off, MOBA_BLOCK), g * 64:(g + 1) * 64]
        vj = kv_ref[pl.ds(off, MOBA_BLOCK), 128 + g * 64:128 + (g + 1) * 64]
        s = _qk(qh, kj) * scale - slopes[h] * dist0f
        s = jnp.where(dist0 >= 0, s, NEG)
        m, l, acc = _flash_update(carry, s, vj)
        o_ref[:, h * 64:(h + 1) * 64] = (acc / l).astype(BF16)


def _moba_prompt(qm, kvb, kmean, batch, seq):
    nb = seq // MOBA_BLOCK
    return pl.pallas_call(
        functools.partial(_moba_prompt_kernel, nb=nb),
        grid=(batch, nb),
        in_specs=[pl.BlockSpec((MOBA_BLOCK, 512), lambda b, i: (b * nb + i, 0)),
                  pl.BlockSpec((seq, 256), lambda b, i: (b, 0)),
                  pl.BlockSpec((1, LANE, 128), lambda b, i: (b, 0, 0))],
        out_specs=pl.BlockSpec((MOBA_BLOCK, 512), lambda b, i: (b * nb + i, 0)),
        out_shape=jax.ShapeDtypeStruct((batch * seq, 512), BF16),
        compiler_params=_cparams(("arbitrary", "arbitrary")),
        name="moba_prompt",
    )(qm, kvb, kmean)


def _mla_prompt_kernel(q_ref, k_ref, o_ref):
    i = pl.program_id(1)
    tq = TOKEN_TILE
    scale = (MLA_NOPE_DIM + MLA_ROPE_DIM) ** -0.5
    dist0 = _dist_tile(tq, tq)
    for h in range(MLA_HEADS):
        qh = q_ref[:, h * 256:(h + 1) * 256]

        def body(j, carry, qh=qh):
            off = pl.multiple_of(j * tq, tq)
            kj = k_ref[pl.ds(off, tq), :]
            return _flash_update(carry, _qk(qh, kj) * scale, kj[:, 0:128])

        carry = lax.fori_loop(0, i, body, _flash_init(tq, 128))
        off = pl.multiple_of(i * tq, tq)
        kj = k_ref[pl.ds(off, tq), :]
        s = jnp.where(dist0 >= 0, _qk(qh, kj) * scale, NEG)
        m, l, acc = _flash_update(carry, s, kj[:, 0:128])
        o_ref[:, h * 128:(h + 1) * 128] = (acc / l).astype(BF16)


def _mla_prompt(qfull, kfull, batch, seq):
    nq = seq // TOKEN_TILE
    return pl.pallas_call(
        _mla_prompt_kernel,
        grid=(batch, nq),
        in_specs=[pl.BlockSpec((TOKEN_TILE, 2048), lambda b, i: (b * nq + i, 0)),
                  pl.BlockSpec((seq, 256), lambda b, i: (b, 0))],
        out_specs=pl.BlockSpec((TOKEN_TILE, 1024), lambda b, i: (b * nq + i, 0)),
        out_shape=jax.ShapeDtypeStruct((batch * seq, 1024), BF16),
        compiler_params=_cparams(("arbitrary", "arbitrary")),
        name="mla_prompt",
    )(qfull, kfull)


def _lambda_full(lam_ref, lam_init):
    lam = lam_ref[...]
    a = jnp.sum(lam[0:1] * lam[1:2], axis=-1, keepdims=True)
    b = jnp.sum(lam[2:3] * lam[3:4], axis=-1, keepdims=True)
    return jnp.exp(a) - jnp.exp(b) + lam_init


def _diff_prompt_kernel(q_ref, k_ref, v_ref, lam_ref, subg_ref, o_ref, *, lam_init):
    i = pl.program_id(1)
    tq = TOKEN_TILE
    scale = DIFF_HEAD_DIM ** -0.5
    slopes = _alibi_slopes(DIFF_HEADS)
    dist0 = _dist_tile(tq, tq)
    dist0f = dist0.astype(F32)
    lam_full = _lambda_full(lam_ref, lam_init)
    rep = DIFF_HEADS // DIFF_KV_HEADS
    for h in range(DIFF_HEADS):
        g = h // rep
        outs = []
        for c in range(2):
            qh = q_ref[:, h * 128 + c * 64:h * 128 + (c + 1) * 64]
            kc = (g * 2 + c) * 64

            def body(j, carry, qh=qh, kc=kc, g=g, slope=slopes[h]):
                off = pl.multiple_of(j * tq, tq)
                kj = k_ref[pl.ds(off, tq), kc:kc + 64]
                vj = v_ref[pl.ds(off, tq), g * 128:(g + 1) * 128]
                s = _qk(qh, kj) * scale - slope * (dist0 + (i - j) * tq).astype(F32)
                return _flash_update(carry, s, vj)

            carry = lax.fori_loop(0, i, body, _flash_init(tq, 128))
            off = pl.multiple_of(i * tq, tq)
            kj = k_ref[pl.ds(off, tq), kc:kc + 64]
            vj = v_ref[pl.ds(off, tq), g * 128:(g + 1) * 128]
            s = _qk(qh, kj) * scale - slopes[h] * dist0f
            s = jnp.where(dist0 >= 0, s, NEG)
            m, l, acc = _flash_update(carry, s, vj)
            outs.append(acc / l)
        o = outs[0] - lam_full * outs[1]
        o = _rms(o, subg_ref[...]) * (1.0 - lam_init)
        o_ref[:, h * 128:(h + 1) * 128] = o.astype(BF16)


def _diff_prompt(q, kb, vb, lam, subg, lam_init, batch, seq):
    nq = seq // TOKEN_TILE
    return pl.pallas_call(
        functools.partial(_diff_prompt_kernel, lam_init=lam_init),
        grid=(batch, nq),
        in_specs=[pl.BlockSpec((TOKEN_TILE, 1024), lambda b, i: (b * nq + i, 0)),
                  pl.BlockSpec((seq, 256), lambda b, i: (b, 0)),
                  pl.BlockSpec((seq, 256), lambda b, i: (b, 0)),
                  pl.BlockSpec((4, 64), lambda b, i: (0, 0)),
                  pl.BlockSpec((1, 128), lambda b, i: (0, 0))],
        out_specs=pl.BlockSpec((TOKEN_TILE, 1024), lambda b, i: (b * nq + i, 0)),
        out_shape=jax.ShapeDtypeStruct((batch * seq, 1024), BF16),
        compiler_params=_cparams(("arbitrary", "arbitrary")),
        name="diff_prompt",
    )(q, kb, vb, lam, subg)


def _page_copy(src_hbm, page, dst, sem):
    return pltpu.make_async_copy(src_hbm.at[page], dst, sem)


def _paged_pipeline(pt_ref, n_pages, layer_off, streams, sem):
    s = pl.program_id(0)
    ns = pl.num_programs(0)
    slot = s % 2

    def start(seq, slot_):
        for p in range(n_pages):
            page = pt_ref[seq * n_pages + p] + layer_off
            for hbm, buf, dst_fn in streams:
                _page_copy(hbm, page, dst_fn(buf, slot_, p), sem.at[slot_]).start()

    @pl.when(s == 0)
    def _():
        start(0, 0)

    @pl.when(s + 1 < ns)
    def _():
        start(s + 1, 1 - slot)

    for p in range(n_pages):
        for hbm, buf, dst_fn in streams:
            _page_copy(hbm, 0, dst_fn(buf, slot, p), sem.at[slot]).wait()
    return slot


def _lane_window(buf, slot, p):
    return buf.at[slot, :, pl.ds(p * LANE, LANE)]


def _row_window(rows):
    return lambda buf, slot, p: buf.at[slot, pl.ds(p * rows, rows), :]


def _moba_decode_kernel(pt_ref, q_ref, knew_ref, vnew_ref, k_hbm, v_hbm, o_ref, kbuf, vbuf, sem, *,
                        n_pages, layer_off, past_len):
    slot = _paged_pipeline(pt_ref, n_pages, layer_off, [(k_hbm, kbuf, _lane_window), (v_hbm, vbuf, _lane_window)],
                           sem)
    scale = MOBA_HEAD_DIM ** -0.5
    nblk = past_len // MOBA_BLOCK
    q = q_ref[0]
    kt = kbuf[slot].astype(BF16)
    s_raw = jnp.dot(q, kt, preferred_element_type=F32)
    lane = lax.broadcasted_iota(jnp.int32, (MOBA_HEADS, LANE), 1)
    gate = jnp.full((MOBA_HEADS, LANE), -jnp.inf, F32)
    for b in range(nblk):
        gs = jnp.sum(s_raw[:, b * MOBA_BLOCK:(b + 1) * MOBA_BLOCK], axis=-1, keepdims=True)
        gate = jnp.where(lane == b, gs, gate)
    rank = jnp.zeros((MOBA_HEADS, LANE), jnp.int32)
    for mb in range(nblk):
        gm = gate[:, mb:mb + 1]
        beats = (gm > gate) | ((gm == gate) & (lane > mb))
        rank = rank + beats.astype(jnp.int32)
    sel = (lane < nblk) & (rank < MOBA_TOPK)
    selmask = jnp.concatenate(
        [jnp.broadcast_to(jnp.sum(jnp.where((lane == b) & sel, 1.0, 0.0), axis=-1, keepdims=True) > 0.5,
                          (MOBA_HEADS, MOBA_BLOCK)) for b in range(nblk)], axis=1)
    slope = _slope_column(_alibi_slopes(MOBA_HEADS))
    pos = lax.broadcasted_iota(jnp.int32, (MOBA_HEADS, past_len), 1)
    dist = (past_len - pos).astype(F32)
    s = jnp.where(selmask, s_raw * scale - slope * dist, NEG)
    qf = q.astype(F32)
    s_own = jnp.sum(qf * knew_ref[0], axis=-1, keepdims=True) * scale
    m = jnp.maximum(jnp.max(s, axis=-1, keepdims=True), s_own)
    p = jnp.exp(s - m)
    p_own = jnp.exp(s_own - m)
    l = jnp.sum(p, axis=-1, keepdims=True) + p_own
    vt = vbuf[slot].astype(BF16)
    acc = _qk(p.astype(BF16), vt) + p_own * vnew_ref[0]
    o_ref[0] = acc / l


def _moba_decode(pt_flat, q_bd, knew, vnew, k_pages, v_pages, layer, n_pool, n_seq, n_pages):
    past_len = n_pages * LANE
    kern = functools.partial(_moba_decode_kernel, n_pages=n_pages, layer_off=layer * n_pool, past_len=past_len)
    return pl.pallas_call(
        kern,
        grid_spec=pltpu.PrefetchScalarGridSpec(
            num_scalar_prefetch=1,
            grid=(n_seq,),
            in_specs=[pl.BlockSpec((1, 8, 128), lambda s, pt: (s, 0, 0)),
                      pl.BlockSpec((1, 1, 128), lambda s, pt: (s, 0, 0)),
                      pl.BlockSpec((1, 1, 128), lambda s, pt: (s, 0, 0)),
                      pl.BlockSpec(memory_space=pl.ANY), pl.BlockSpec(memory_space=pl.ANY)],
            out_specs=pl.BlockSpec((1, 8, 128), lambda s, pt: (s, 0, 0)),
            scratch_shapes=[pltpu.VMEM((2, 128, past_len), F32), pltpu.VMEM((2, 128, past_len), F32),
                            pltpu.SemaphoreType.DMA((2,))]),
        out_shape=jax.ShapeDtypeStruct((n_seq, 8, 128), F32),
        compiler_params=_cparams(("arbitrary",)),
        name="moba_decode",
    )(pt_flat, q_bd, knew, vnew, k_pages, v_pages)


def _mla_decode_kernel(pt_ref, qlat_ref, qpe_ref, cnew_ref, pnew_ref, c_hbm, p_hbm, o_ref, cbuf, pbuf, sem, *,
                       n_pages, layer_off):
    slot = _paged_pipeline(pt_ref, n_pages, layer_off,
                           [(c_hbm, cbuf, _row_window(LANE)), (p_hbm, pbuf, _lane_window)], sem)
    scale = (MLA_NOPE_DIM + MLA_ROPE_DIM) ** -0.5
    qlat = qlat_ref[0]
    qpe = qpe_ref[0]
    ckv = cbuf[slot].astype(BF16)
    kpet = pbuf[slot].astype(BF16)
    s = (_qk(qlat, ckv) + jnp.dot(qpe, kpet, preferred_element_type=F32)) * scale
    s_own = (jnp.sum(qlat.astype(F32) * cnew_ref[0], axis=-1, keepdims=True)
             + jnp.sum(qpe.astype(F32) * pnew_ref[0], axis=-1, keepdims=True)) * scale
    m = jnp.maximum(jnp.max(s, axis=-1, keepdims=True), s_own)
    p = jnp.exp(s - m)
    p_own = jnp.exp(s_own - m)
    l = jnp.sum(p, axis=-1, keepdims=True) + p_own
    acc = jnp.dot(p.astype(BF16), ckv, preferred_element_type=F32) + p_own * cnew_ref[0]
    o_ref[0] = (acc / l).astype(BF16)


def _mla_decode(pt_flat, qlat, qpe, cnew, pnew, c_pages, p_pages, layer, n_pool, n_seq, n_pages):
    past_len = n_pages * LANE
    kern = functools.partial(_mla_decode_kernel, n_pages=n_pages, layer_off=layer * n_pool)
    return pl.pallas_call(
        kern,
        grid_spec=pltpu.PrefetchScalarGridSpec(
            num_scalar_prefetch=1,
            grid=(n_seq,),
            in_specs=[pl.BlockSpec((1, 8, 128), lambda s, pt: (s, 0, 0)),
                      pl.BlockSpec((1, 8, 32), lambda s, pt: (s, 0, 0)),
                      pl.BlockSpec((1, 1, 128), lambda s, pt: (s, 0, 0)),
                      pl.BlockSpec((1, 1, 32), lambda s, pt: (s, 0, 0)),
                      pl.BlockSpec(memory_space=pl.ANY), pl.BlockSpec(memory_space=pl.ANY)],
            out_specs=pl.BlockSpec((1, 8, 128), lambda s, pt: (s, 0, 0)),
            scratch_shapes=[pltpu.VMEM((2, past_len, 128), F32), pltpu.VMEM((2, MLA_ROPE_DIM, past_len), F32),
                            pltpu.SemaphoreType.DMA((2,))]),
        out_shape=jax.ShapeDtypeStruct((n_seq, 8, 128), BF16),
        compiler_params=_cparams(("arbitrary",)),
        name="mla_decode",
    )(pt_flat, qlat, qpe, cnew, pnew, c_pages, p_pages)


def _diff_decode_kernel(pt_ref, q_ref, knew_ref, vnew_ref, lam_ref, subg_ref, k_hbm, v_hbm, o_ref,
                        kbuf, vbuf, sem, m_ref, l_ref, acc_ref, *, n_pages, pages_per_step, layer_off, lam_init):
    st = pl.program_id(0)
    halves = n_pages // pages_per_step
    half = st % halves
    slot = _paged_pipeline(pt_ref, pages_per_step, layer_off,
                           [(k_hbm, kbuf, _lane_window), (v_hbm, vbuf, _row_window(2 * LANE))], sem)
    scale = DIFF_HEAD_DIM ** -0.5
    chunk = pages_per_step * LANE
    q = q_ref[0]
    kt = kbuf[slot].astype(BF16)
    head_slopes = _alibi_slopes(DIFF_HEADS)
    slope = _slope_column([head_slopes[(r // 8) * 4 + (r % 4)] for r in range(16)])
    pos = lax.broadcasted_iota(jnp.int32, (16, chunk), 1) + half * chunk
    dist = (n_pages * LANE - pos).astype(F32)
    s = jnp.dot(q, kt, preferred_element_type=F32) * scale - slope * dist

    @pl.when(half == 0)
    def _():
        m_ref[...] = jnp.full_like(m_ref, NEG)
        l_ref[...] = jnp.zeros_like(l_ref)
        acc_ref[...] = jnp.zeros_like(acc_ref)

    m_old = m_ref[...]
    m_new = jnp.maximum(m_old, jnp.max(s, axis=-1, keepdims=True))
    alpha = jnp.exp(m_old - m_new)
    p = jnp.exp(s - m_new)
    l_ref[...] = alpha * l_ref[...] + jnp.sum(p, axis=-1, keepdims=True)
    pb = p.astype(BF16)
    pv = []
    for g in range(DIFF_KV_HEADS):
        vg = vbuf[slot, pl.ds(g, chunk, stride=2), :].astype(BF16)
        pv.append(jnp.dot(pb[g * 8:(g + 1) * 8], vg, preferred_element_type=F32))
    acc_ref[...] = alpha * acc_ref[...] + jnp.concatenate(pv, axis=0)
    m_ref[...] = m_new

    @pl.when(half == halves - 1)
    def _():
        s_own = jnp.sum(q.astype(F32) * knew_ref[0], axis=-1, keepdims=True) * scale
        m_o = m_ref[...]
        m_f = jnp.maximum(m_o, s_own)
        a = jnp.exp(m_o - m_f)
        p_own = jnp.exp(s_own - m_f)
        l = a * l_ref[...] + p_own
        vnew = vnew_ref[0]
        vrows = jnp.concatenate([jnp.broadcast_to(vnew[:, g * 128:(g + 1) * 128], (8, 128))
                                 for g in range(DIFF_KV_HEADS)], axis=0)
        o = (a * acc_ref[...] + p_own * vrows) / l
        lam_full = _lambda_full(lam_ref, lam_init)
        res = []
        for g in range(DIFF_KV_HEADS):
            o1 = o[g * 8:g * 8 + 4]
            o2 = o[g * 8 + 4:g * 8 + 8]
            res.append(_rms(o1 - lam_full * o2, subg_ref[...]) * (1.0 - lam_init))
        o_ref[0] = jnp.concatenate(res, axis=0).astype(BF16)


def _diff_decode(pt_flat, q_bd, knew, vnew, lam, subg, k_pages, v_pages, layer, n_pool, n_seq, n_pages, lam_init):
    halves = 2
    pps = n_pages // halves
    chunk = pps * LANE
    kern = functools.partial(_diff_decode_kernel, n_pages=n_pages, pages_per_step=pps, layer_off=layer * n_pool,
                             lam_init=lam_init)
    return pl.pallas_call(
        kern,
        grid_spec=pltpu.PrefetchScalarGridSpec(
            num_scalar_prefetch=1,
            grid=(n_seq * halves,),
            in_specs=[pl.BlockSpec((1, 16, 256), lambda s, pt: (s // halves, 0, 0)),
                      pl.BlockSpec((1, 1, 256), lambda s, pt: (s // halves, 0, 0)),
                      pl.BlockSpec((1, 1, 256), lambda s, pt: (s // halves, 0, 0)),
                      pl.BlockSpec((4, 64), lambda s, pt: (0, 0)),
                      pl.BlockSpec((1, 128), lambda s, pt: (0, 0)),
                      pl.BlockSpec(memory_space=pl.ANY), pl.BlockSpec(memory_space=pl.ANY)],
            out_specs=pl.BlockSpec((1, 8, 128), lambda s, pt: (s // halves, 0, 0)),
            scratch_shapes=[pltpu.VMEM((2, 256, chunk), F32), pltpu.VMEM((2, 2 * chunk, 128), F32),
                            pltpu.SemaphoreType.DMA((2,)),
                            pltpu.VMEM((16, 1), F32), pltpu.VMEM((16, 1), F32), pltpu.VMEM((16, 128), F32)]),
        out_shape=jax.ShapeDtypeStruct((n_seq, 8, 128), BF16),
        compiler_params=_cparams(("arbitrary",)),
        name="diff_decode",
    )(pt_flat, q_bd, knew, vnew, lam, subg, k_pages, v_pages)


def _rope_tables(pos):
    half = MLA_ROPE_DIM // 2
    inv = ROPE_BASE ** (-jnp.arange(half, dtype=F32) / half)
    ang = pos.astype(F32)[:, None] * inv[None, :]
    pad = jnp.zeros((pos.shape[0], LANE - MLA_ROPE_DIM), F32)
    cos, sin = jnp.cos(ang), jnp.sin(ang)
    return jnp.concatenate([cos, cos, pad], axis=1), jnp.concatenate([sin, sin, pad], axis=1)


def _swap_halves(w):
    half = w.shape[-1] // 2
    return jnp.concatenate([-w[..., half:], w[..., :half]], axis=-1)


def _pad_cols(w, width):
    return jnp.pad(w, [(0, 0)] * (w.ndim - 1) + [(0, width - w.shape[-1])])


def _ab_weights(w_in, w_uq, w_uk, w_uv):
    kpe_w = w_in[:, 1152:1184]
    w1 = jnp.concatenate([w_in[:, :1152], _pad_cols(kpe_w, LANE), _pad_cols(_swap_halves(kpe_w), LANE)], axis=1)
    uq = w_uq.reshape(MLA_Q_RANK, MLA_HEADS, MLA_NOPE_DIM + MLA_ROPE_DIM)
    nope = uq[:, :, :MLA_NOPE_DIM].reshape(MLA_Q_RANK, MLA_HEADS * MLA_NOPE_DIM)
    pe = uq[:, :, MLA_NOPE_DIM:]
    pe_pad = _pad_cols(pe, LANE).reshape(MLA_Q_RANK, MLA_HEADS * LANE)
    pes_pad = _pad_cols(_swap_halves(pe), LANE).reshape(MLA_Q_RANK, MLA_HEADS * LANE)
    wuq = jnp.concatenate([nope, pe_pad, pes_pad], axis=1)
    wukt = jnp.transpose(w_uk, (1, 2, 0))
    eye = jnp.eye(MLA_HEADS, dtype=F32)
    wuv_bd = (jnp.transpose(w_uv, (1, 0, 2))[:, :, None, :] * eye[:, None, :, None]).reshape(
        MLA_HEADS * MLA_KV_RANK, MLA_HEADS * MLA_V_DIM)
    return w1.astype(BF16), wuq.astype(BF16), wukt.astype(BF16), wuv_bd.astype(BF16)


def _block_diag_rows(q, blocks, width):
    n, rows, _ = q.shape
    nblk = max(blocks) + 1
    sel = jnp.asarray([[1.0 if blocks[r] == b else 0.0 for b in range(nblk)] for r in range(rows)], q.dtype)
    return (q[:, :, None, :] * sel[None, :, :, None]).reshape(n, rows, nblk * width)


def kernel(x_prompt, x_sample, c_prompt, c_sample, cache_moba_k, cache_moba_v, cache_mla_ckv, cache_mla_kpe,
           cache_diff_k, cache_diff_v, page_table, mod_w, mod_b, norm_mix, norm_ffn, ab_w_in, mla_q_norm,
           mla_kv_norm, mla_w_uq, mla_w_uk, mla_w_uv, ab_w_out, c_w_in, diff_lambda, diff_subln, c_w_out,
           moe_w_group, moe_b_group, moe_w_expert, moe_b_expert, moe_w1, moe_w3, moe_w2, final_norm):
    batch, seq, d = x_prompt.shape
    n_seq = x_sample.shape[0]
    depth = mod_w.shape[0]
    n_pool = cache_moba_k.shape[1]
    n_pages = page_table.shape[1]
    past_len = n_pages * cache_moba_k.shape[2]
    assert d == D_MODEL and x_sample.shape[1] == 1 and cache_moba_k.shape[2] == LANE
    assert seq % MOBA_BLOCK == 0 and past_len % MOBA_BLOCK == 0 and n_seq % 8 == 0
    n_p = batch * seq
    tpb = seq // TOKEN_TILE
    moe_tile = math.gcd(seq, MOE_TILE)

    xp = x_prompt.reshape(n_p, d)
    xs = x_sample.reshape(n_seq, d)
    mod = _modulation(jnp.concatenate([c_prompt, c_sample], axis=0), mod_w, mod_b)
    mod = mod.reshape(depth, batch + n_seq, 6, d)
    pt_flat = page_table.reshape(-1)

    n_all = cache_moba_k.shape[0] * n_pool
    mk_pages = jnp.transpose(cache_moba_k, (0, 1, 3, 4, 2)).reshape(n_all, 128, LANE)
    mv_pages = jnp.transpose(cache_moba_v, (0, 1, 3, 4, 2)).reshape(n_all, 128, LANE)
    ckv_pages = cache_mla_ckv.reshape(n_all, LANE, 128)
    kpe_pages = jnp.transpose(cache_mla_kpe, (0, 1, 3, 2)).reshape(n_all, MLA_ROPE_DIM, LANE)
    n_all_c = cache_diff_k.shape[0] * n_pool
    dk_pages = jnp.transpose(cache_diff_k, (0, 1, 3, 4, 5, 2)).reshape(n_all_c, 256, LANE)
    dv_pages = cache_diff_v.reshape(n_all_c, 2 * LANE, 128)

    cos_p, sin_p = _rope_tables(jnp.arange(seq, dtype=jnp.int32))
    cos_s, sin_s = _rope_tables(jnp.full((1,), past_len, jnp.int32))

    w1_all = moe_w1.astype(BF16)
    w3_all = moe_w3.astype(BF16)
    w2_all = moe_w2.astype(BF16)

    moba_blocks = [h // (MOBA_HEADS // MOBA_KV_HEADS) for h in range(MOBA_HEADS)]
    diff_rows = [(g, c, j) for g in range(DIFF_KV_HEADS) for c in range(2) for j in range(4)]

    new_p = {k: [] for k in ("mk", "mv", "ckv", "kpe", "dk", "dv")}
    new_s = {k: [] for k in ("mk", "mv", "ckv", "kpe", "dk", "dv")}
    for l in range(depth):
        i = l // 2
        mp = [mod[l, :batch, k].reshape(batch, 1, d) for k in range(6)]
        ms = [mod[l, batch:, k] for k in range(6)]
        g_mix = norm_mix[l].reshape(1, d)
        if l % 2 == 0:
            w1, wuq, wukt, wuv_bd = _ab_weights(ab_w_in[i], mla_w_uq[i], mla_w_uk[i], mla_w_uv[i])
            wout = ab_w_out[i].astype(BF16)
            qg = mla_q_norm[i].reshape(1, -1)
            kvg = mla_kv_norm[i].reshape(1, -1)
            qm, mk, mv, kvb, kmean, qfull, ckv, kpe, kfull = _inproj_ab(
                xp, mp[0], mp[1], g_mix, w1, qg, kvg, wuq, wukt, cos_p, sin_p, False, TOKEN_TILE, tpb)
            kmean = jnp.pad(kmean.reshape(batch, tpb, 128), ((0, 0), (0, LANE - tpb), (0, 0)))
            oa = _moba_prompt(qm, kvb, kmean, batch, seq)
            olat = _mla_prompt(qfull, kfull, batch, seq)
            xp = _outproj_ab(xp, mp[2], oa, olat, wuv_bd, wout, False, TOKEN_TILE, tpb)
            new_p["mk"].append(mk); new_p["mv"].append(mv); new_p["ckv"].append(ckv); new_p["kpe"].append(kpe)
            qm, mk, mv, kvb, kmean, qfull, ckv, kpe, kfull = _inproj_ab(
                xs, ms[0], ms[1], g_mix, w1, qg, kvg, wuq, wukt, cos_s, sin_s, True, n_seq, 1)
            q_bd = _block_diag_rows(qm.reshape(n_seq, MOBA_HEADS, 64), moba_blocks, 64)
            o_raw = _moba_decode(pt_flat, q_bd, mk.reshape(n_seq, 1, 128), mv.reshape(n_seq, 1, 128),
                                 mk_pages, mv_pages, i, n_pool, n_seq, n_pages)
            o_raw = o_raw.reshape(n_seq, MOBA_KV_HEADS, 4, MOBA_KV_HEADS, 64)
            oa = jnp.concatenate([o_raw[:, g, :, g, :] for g in range(MOBA_KV_HEADS)], axis=1)
            oa = oa.reshape(n_seq, 512).astype(BF16)
            qf3 = qfull.reshape(n_seq, MLA_HEADS, 256)
            olat = _mla_decode(pt_flat, qf3[:, :, :128], qf3[:, :, 128:128 + MLA_ROPE_DIM],
                               ckv.reshape(n_seq, 1, 128), kpe.reshape(n_seq, 1, MLA_ROPE_DIM),
                               ckv_pages, kpe_pages, i, n_pool, n_seq, n_pages)
            xs = _outproj_ab(xs, ms[2], oa, olat.reshape(n_seq, 1024), wuv_bd, wout, True, n_seq, 1)
            new_s["mk"].append(mk); new_s["mv"].append(mv); new_s["ckv"].append(ckv); new_s["kpe"].append(kpe)
        else:
            lam_init = 0.8 - 0.6 * math.exp(-0.3 * l)
            wc = c_w_in[i].astype(BF16)
            wout = c_w_out[i].astype(BF16)
            lam = diff_lambda[i]
            subg = diff_subln[i].reshape(1, -1)
            q, k, v, kb, vb = _inproj_c(xp, mp[0], mp[1], g_mix, wc, False, TOKEN_TILE, tpb)
            o = _diff_prompt(q, kb, vb, lam, subg, lam_init, batch, seq)
            xp = _outproj_c(xp, mp[2], o, wout, False, TOKEN_TILE, tpb)
            new_p["dk"].append(k); new_p["dv"].append(v)
            q, k, v, kb, vb = _inproj_c(xs, ms[0], ms[1], g_mix, wc, True, n_seq, 1)
            q4 = q.reshape(n_seq, DIFF_HEADS, 2, 64)
            q_rows = jnp.stack([q4[:, g * 4 + j, c, :] for (g, c, j) in diff_rows], axis=1)
            q_bd = _block_diag_rows(q_rows, [g * 2 + c for (g, c, j) in diff_rows], 64)
            o = _diff_decode(pt_flat, q_bd, k.reshape(n_seq, 1, 256), v.reshape(n_seq, 1, 256), lam, subg,
                             dk_pages, dv_pages, i, n_pool, n_seq, n_pages, lam_init)
            xs = _outproj_c(xs, ms[2], o.reshape(n_seq, 1024), wout, True, n_seq, 1)
            new_s["dk"].append(k); new_s["dv"].append(v)
        wr = jnp.concatenate([moe_w_expert[l], moe_w_group[l],
                              jnp.zeros((d, LANE - N_EXPERTS - N_GROUPS), F32)], axis=1)
        br = jnp.concatenate([moe_b_expert[l], moe_b_group[l],
                              jnp.zeros((LANE - N_EXPERTS - N_GROUPS,), F32)]).reshape(1, LANE)
        g_ffn = norm_ffn[l].reshape(1, d)
        hp, gp = _router(xp, mp[3], mp[4], g_ffn, wr, br, False, TOKEN_TILE, tpb)
        xp = _moe(hp, gp, w1_all[l], w3_all[l], w2_all[l], xp, mp[5], False, moe_tile, seq // moe_tile)
        hs, gs = _router(xs, ms[3], ms[4], g_ffn, wr, br, True, n_seq, 1)
        xs = _moe(hs, gs, w1_all[l], w3_all[l], w2_all[l], xs, ms[5], True, n_seq, 1)

    fg = final_norm.reshape(1, d)
    y_prompt = _final_norm(xp, fg, TOKEN_TILE).reshape(batch, seq, d)
    y_sample = _final_norm(xs, fg, n_seq).reshape(n_seq, 1, d)

    def stack(lst, shape):
        return jnp.stack([a.reshape(shape) for a in lst])

    outs_p = (stack(new_p["mk"], (batch, seq, MOBA_KV_HEADS, MOBA_HEAD_DIM)),
              stack(new_p["mv"], (batch, seq, MOBA_KV_HEADS, MOBA_HEAD_DIM)),
              stack(new_p["ckv"], (batch, seq, MLA_KV_RANK)),
              stack(new_p["kpe"], (batch, seq, MLA_ROPE_DIM)),
              stack(new_p["dk"], (batch, seq, DIFF_KV_HEADS, 2, DIFF_HEAD_DIM)),
              stack(new_p["dv"], (batch, seq, DIFF_KV_HEADS, 2 * DIFF_HEAD_DIM)))
    outs_s = (stack(new_s["mk"], (n_seq, 1, MOBA_KV_HEADS, MOBA_HEAD_DIM)),
              stack(new_s["mv"], (n_seq, 1, MOBA_KV_HEADS, MOBA_HEAD_DIM)),
              stack(new_s["ckv"], (n_seq, 1, MLA_KV_RANK)),
              stack(new_s["kpe"], (n_seq, 1, MLA_ROPE_DIM)),
              stack(new_s["dk"], (n_seq, 1, DIFF_KV_HEADS, 2, DIFF_HEAD_DIM)),
              stack(new_s["dv"], (n_seq, 1, DIFF_KV_HEADS, 2 * DIFF_HEAD_DIM)))
    return (y_prompt, y_sample) + outs_p + outs_s
```

```python
import functools
import math

import jax
import jax.numpy as jnp
from jax import lax
from jax.experimental import pallas as pl
from jax.experimental.pallas import tpu as pltpu

F32 = jnp.float32
BF16 = jnp.bfloat16
U32 = jnp.uint32
I32 = jnp.int32
HIGHEST = lax.Precision.HIGHEST

D_MODEL = 1024
MOBA_HEADS, MOBA_KV_HEADS, MOBA_HEAD_DIM, MOBA_BLOCK, MOBA_TOPK = 8, 2, 64, 256, 3
MLA_HEADS, MLA_Q_RANK, MLA_KV_RANK, MLA_NOPE_DIM, MLA_ROPE_DIM, MLA_V_DIM = 8, 256, 128, 64, 32, 64
ROPE_BASE = 10000.0
DIFF_HEADS, DIFF_KV_HEADS, DIFF_HEAD_DIM = 8, 2, 64
N_GROUPS, EXPERTS_PER_GROUP, EXPERT_FF, TOP_EXPERTS = 4, 8, 256, 2
N_EXPERTS = N_GROUPS * EXPERTS_PER_GROUP
EPS = 1e-6
NEG = -1e30

LANE = 128
VMEM_LIMIT = 56 * 1024 * 1024
TOKEN_TILE = 256
EXPERT_TILE = 256
POS_HI_LANE, POS_LO_LANE, SEL_LANE0 = 64, 65, 66
ONES_LANE = 64
MLA_ONES_LANE = MLA_KV_RANK + MLA_ROPE_DIM
AB_IN_EXT = 1024 + 256 + 512 + 256 + 128 + 128 + 128
UQ_EXT = 512 + 1024 + 1024
C_IN_EXT = 2048 + 256 + 256 + 512 + 512


def _cparams(sem):
    return pltpu.CompilerParams(dimension_semantics=sem, vmem_limit_bytes=VMEM_LIMIT)


def _alibi_slopes(n):
    return [2.0 ** (-8.0 * (i + 1) / n) for i in range(n)]


def _slope_column(values):
    row = lax.broadcasted_iota(I32, (len(values), 1), 0)
    col = jnp.zeros((len(values), 1), F32)
    for r, v in enumerate(values):
        col = jnp.where(row == r, v, col)
    return col


def _rms(x, g):
    return x * lax.rsqrt(jnp.mean(x * x, axis=-1, keepdims=True) + EPS) * g


def _ada(x, g, shift, scale):
    return _rms(x, g) * (1.0 + scale) + shift


def _mod_kernel(c_ref, w_ref, b_ref, o_ref):
    c = c_ref[...]
    a = c / (1.0 + jnp.exp(-c))
    o_ref[0] = jnp.dot(a, w_ref[0], preferred_element_type=F32, precision=HIGHEST) + b_ref[0]


def _modulation(c_all, mod_w, mod_b):
    depth, d, n6 = mod_w.shape
    rows = c_all.shape[0]
    tn = 512
    return pl.pallas_call(
        _mod_kernel,
        grid=(depth, n6 // tn),
        in_specs=[
            pl.BlockSpec((rows, d), lambda l, j: (0, 0)),
            pl.BlockSpec((1, d, tn), lambda l, j: (l, 0, j)),
            pl.BlockSpec((1, 1, tn), lambda l, j: (l, 0, j)),
        ],
        out_specs=pl.BlockSpec((1, rows, tn), lambda l, j: (l, 0, j)),
        out_shape=jax.ShapeDtypeStruct((depth, rows, n6), F32),
        compiler_params=_cparams(("arbitrary", "arbitrary")),
        name="modulation",
    )(c_all, mod_w, mod_b.reshape(depth, 1, n6))


def _mod_spec(per_row, tm, tiles_per_batch):
    if per_row:
        return pl.BlockSpec((tm, D_MODEL), lambda t: (t, 0))
    return pl.BlockSpec((None, 1, D_MODEL), lambda t: (t // tiles_per_batch, 0, 0))


def _pos_spec(per_row, tm, width, tiles_per_batch):
    if per_row:
        return pl.BlockSpec((1, width), lambda t: (0, 0))
    return pl.BlockSpec((tm, width), lambda t: (t % tiles_per_batch, 0))


def _full_spec(shape):
    nd = len(shape)
    return pl.BlockSpec(shape, lambda t: (0,) * nd)


def _inproj_ab_kernel(x_ref, sh_ref, sc_ref, g_ref, w1_ref, qg_ref, kvg_ref, wuq_ref, wukt_ref, cos_ref, sin_ref,
                      qconst_ref, ktab_ref, kones_ref,
                      qa_ref, mk_ref, mv_ref, kva_ref, kmean_ref, qfull_ref, ckv_ref, kpe_ref, kfull_ref):
    h = _ada(x_ref[...], g_ref[...], sh_ref[...], sc_ref[...]).astype(BF16)
    z = jnp.dot(h, w1_ref[...], preferred_element_type=F32)
    qa_ref[...] = (z[:, 0:1024] * (MOBA_HEAD_DIM ** -0.5) + qconst_ref[...]).astype(BF16)
    mk = z[:, 1024:1152]
    mk_ref[...] = mk
    mv_ref[...] = z[:, 1152:1280]
    kva_ref[...] = (z[:, 1280:1792] + ktab_ref[...]).astype(BF16)
    kmean_ref[0] = jnp.mean(mk, axis=0, keepdims=True)
    cos = cos_ref[...]
    sin = sin_ref[...]
    ckv = _rms(z[:, 2048:2176], kvg_ref[...])
    ckv_ref[...] = ckv
    kpe = z[:, 2176:2304] * cos + z[:, 2304:2432] * sin
    kpe_ref[...] = kpe[:, 0:MLA_ROPE_DIM]
    kfull_ref[:, 0:128] = ckv.astype(BF16)
    kfull_ref[:, 128:256] = (kpe + kones_ref[...]).astype(BF16)
    cqn = _rms(z[:, 1792:2048], qg_ref[...]).astype(BF16)
    q2 = jnp.dot(cqn, wuq_ref[...], preferred_element_type=F32)
    scale = (MLA_NOPE_DIM + MLA_ROPE_DIM) ** -0.5
    for hd in range(MLA_HEADS):
        nope = q2[:, hd * 64:(hd + 1) * 64].astype(BF16)
        qlat = jnp.dot(nope, wukt_ref[hd], preferred_element_type=F32)
        qfull_ref[hd, :, 0:128] = (qlat * scale).astype(BF16)
        pe = q2[:, 512 + hd * 128:512 + (hd + 1) * 128] * cos + q2[:, 1536 + hd * 128:1536 + (hd + 1) * 128] * sin
        qfull_ref[hd, :, 128:256] = (pe * scale).astype(BF16)


def _inproj_ab(x, shift, scale, g, w1, qg, kvg, wuq, wukt, cos_t, sin_t, qconst, ktab, kones,
               per_row, tm, tiles_per_batch):
    n = x.shape[0]
    nt = n // tm
    row = lambda w: pl.BlockSpec((tm, w), lambda t: (t, 0))
    pos = lambda w: _pos_spec(per_row, tm, w, tiles_per_batch)
    outs = [
        (jax.ShapeDtypeStruct((n, 1024), BF16), row(1024)),
        (jax.ShapeDtypeStruct((n, 128), F32), row(128)),
        (jax.ShapeDtypeStruct((n, 128), F32), row(128)),
        (jax.ShapeDtypeStruct((n, 512), BF16), row(512)),
        (jax.ShapeDtypeStruct((nt, 1, 128), F32), pl.BlockSpec((1, 1, 128), lambda t: (t, 0, 0))),
        (jax.ShapeDtypeStruct((MLA_HEADS, n, 256), BF16), pl.BlockSpec((MLA_HEADS, tm, 256), lambda t: (0, t, 0))),
        (jax.ShapeDtypeStruct((n, 128), F32), row(128)),
        (jax.ShapeDtypeStruct((n, MLA_ROPE_DIM), F32), row(MLA_ROPE_DIM)),
        (jax.ShapeDtypeStruct((n, 256), BF16), row(256)),
    ]
    return pl.pallas_call(
        _inproj_ab_kernel,
        grid=(nt,),
        in_specs=[
            row(D_MODEL), _mod_spec(per_row, tm, tiles_per_batch), _mod_spec(per_row, tm, tiles_per_batch),
            _full_spec((1, D_MODEL)), _full_spec((D_MODEL, AB_IN_EXT)), _full_spec((1, MLA_Q_RANK)),
            _full_spec((1, MLA_KV_RANK)), _full_spec((MLA_Q_RANK, UQ_EXT)), _full_spec((MLA_HEADS, 64, 128)),
            pos(LANE), pos(LANE), _full_spec((1, 1024)), pos(512), _full_spec((1, LANE)),
        ],
        out_specs=[o[1] for o in outs],
        out_shape=[o[0] for o in outs],
        compiler_params=_cparams(("arbitrary",)),
        name="inproj_ab",
    )(x, shift, scale, g, w1, qg, kvg, wuq, wukt, cos_t, sin_t, qconst, ktab, kones)


def _inproj_c_kernel(x_ref, sh_ref, sc_ref, g_ref, w_ref, qconst_ref, ktab_ref, qa_ref, k_ref, v_ref, ka_ref, va_ref):
    h = _ada(x_ref[...], g_ref[...], sh_ref[...], sc_ref[...]).astype(BF16)
    z = jnp.dot(h, w_ref[...], preferred_element_type=F32)
    qa_ref[...] = (z[:, 0:2048] * (DIFF_HEAD_DIM ** -0.5) + qconst_ref[...]).astype(BF16)
    k_ref[...] = z[:, 2048:2304]
    v_ref[...] = z[:, 2304:2560]
    ka_ref[...] = (z[:, 2560:3072] + ktab_ref[:, 0:512]).astype(BF16)
    va_ref[...] = (z[:, 3072:3584] + ktab_ref[:, 512:1024]).astype(BF16)


def _inproj_c(x, shift, scale, g, w, qconst, ktab, per_row, tm, tiles_per_batch):
    n = x.shape[0]
    row = lambda wd: pl.BlockSpec((tm, wd), lambda t: (t, 0))
    return pl.pallas_call(
        _inproj_c_kernel,
        grid=(n // tm,),
        in_specs=[row(D_MODEL), _mod_spec(per_row, tm, tiles_per_batch), _mod_spec(per_row, tm, tiles_per_batch),
                  _full_spec((1, D_MODEL)), _full_spec((D_MODEL, C_IN_EXT)), _full_spec((1, 2048)),
                  _pos_spec(per_row, tm, 1024, tiles_per_batch)],
        out_specs=[row(2048), row(256), row(256), row(512), row(512)],
        out_shape=[jax.ShapeDtypeStruct((n, 2048), BF16), jax.ShapeDtypeStruct((n, 256), F32),
                   jax.ShapeDtypeStruct((n, 256), F32), jax.ShapeDtypeStruct((n, 512), BF16),
                   jax.ShapeDtypeStruct((n, 512), BF16)],
        compiler_params=_cparams(("arbitrary",)),
        name="inproj_c",
    )(x, shift, scale, g, w, qconst, ktab)


def _outproj_ab_kernel(x_ref, gate_ref, oa_ref, olat_ref, wuv_ref, wout_ref, o_ref):
    ob = jnp.dot(olat_ref[...], wuv_ref[...], preferred_element_type=F32).astype(BF16)
    y = jnp.dot(oa_ref[...], wout_ref[0:512, :], preferred_element_type=F32)
    y = y + jnp.dot(ob, wout_ref[512:1024, :], preferred_element_type=F32)
    o_ref[...] = x_ref[...] + gate_ref[...] * y


def _outproj_ab(x, gate, oa, olat, wuv_bd, wout, per_row, tm, tiles_per_batch):
    n = x.shape[0]
    row = lambda wd: pl.BlockSpec((tm, wd), lambda t: (t, 0))
    return pl.pallas_call(
        _outproj_ab_kernel,
        grid=(n // tm,),
        in_specs=[row(D_MODEL), _mod_spec(per_row, tm, tiles_per_batch), row(512), row(1024),
                  _full_spec((1024, 512)), _full_spec((1024, D_MODEL))],
        out_specs=row(D_MODEL),
        out_shape=jax.ShapeDtypeStruct((n, D_MODEL), F32),
        compiler_params=_cparams(("arbitrary",)),
        name="outproj_ab",
    )(x, gate, oa, olat, wuv_bd, wout)


def _outproj_c_kernel(x_ref, gate_ref, o_in_ref, wout_ref, o_ref):
    y = jnp.dot(o_in_ref[...], wout_ref[...], preferred_element_type=F32)
    o_ref[...] = x_ref[...] + gate_ref[...] * y


def _outproj_c(x, gate, o_in, wout, per_row, tm, tiles_per_batch):
    n = x.shape[0]
    row = lambda wd: pl.BlockSpec((tm, wd), lambda t: (t, 0))
    return pl.pallas_call(
        _outproj_c_kernel,
        grid=(n // tm,),
        in_specs=[row(D_MODEL), _mod_spec(per_row, tm, tiles_per_batch), row(1024), _full_spec((1024, D_MODEL))],
        out_specs=row(D_MODEL),
        out_shape=jax.ShapeDtypeStruct((n, D_MODEL), F32),
        compiler_params=_cparams(("arbitrary",)),
        name="outproj_c",
    )(x, gate, o_in, wout)


def _final_norm_kernel(x_ref, g_ref, o_ref):
    o_ref[...] = _rms(x_ref[...], g_ref[...])


def _final_norm(x, g, tm):
    n = x.shape[0]
    return pl.pallas_call(
        _final_norm_kernel,
        grid=(n // tm,),
        in_specs=[pl.BlockSpec((tm, D_MODEL), lambda t: (t, 0)), _full_spec((1, D_MODEL))],
        out_specs=pl.BlockSpec((tm, D_MODEL), lambda t: (t, 0)),
        out_shape=jax.ShapeDtypeStruct((n, D_MODEL), F32),
        compiler_params=_cparams(("arbitrary",)),
        name="final_norm",
    )(x, g)


def _router_kernel(x_ref, sh_ref, sc_ref, g_ref, wr_ref, br_ref, hp_ref, info_ref):
    h = _ada(x_ref[...], g_ref[...], sh_ref[...], sc_ref[...])
    bits = lax.bitcast_convert_type(h.astype(BF16).astype(F32), U32)
    hp_ref[...] = bits[:, 0:512] | (bits[:, 512:1024] >> 16)
    logit = jnp.dot(h, wr_ref[...], preferred_element_type=F32, precision=HIGHEST) + br_ref[...]
    lane = lax.broadcasted_iota(I32, logit.shape, 1)
    is_grp = (lane >= N_EXPERTS) & (lane < N_EXPERTS + N_GROUPS)
    lg = jnp.where(is_grp, logit, -jnp.inf)
    mg = jnp.max(lg, axis=-1, keepdims=True)
    g_w = 1.0 / jnp.sum(jnp.exp(lg - mg), axis=-1, keepdims=True)
    g_idx = jnp.min(jnp.where(lg == mg, lane, 4 * LANE), axis=-1, keepdims=True) - N_EXPERTS
    in_grp = (lane < N_EXPERTS) & ((lane // EXPERTS_PER_GROUP) == g_idx)
    le = jnp.where(in_grp, logit, -jnp.inf)
    m1 = jnp.max(le, axis=-1, keepdims=True)
    i1 = jnp.min(jnp.where(le == m1, lane, 4 * LANE), axis=-1, keepdims=True)
    le2 = jnp.where(lane == i1, -jnp.inf, le)
    m2 = jnp.max(le2, axis=-1, keepdims=True)
    i2 = jnp.min(jnp.where(le2 == m2, lane, 4 * LANE), axis=-1, keepdims=True)
    r = jnp.exp(m2 - m1)
    w1 = g_w / (1.0 + r)
    w2 = g_w * r / (1.0 + r)
    info = jnp.where(lane == 0, i1.astype(F32), 0.0) + jnp.where(lane == 1, i2.astype(F32), 0.0)
    info_ref[...] = info + jnp.where(lane == 2, w1, 0.0) + jnp.where(lane == 3, w2, 0.0)


def _router(x, shift, scale, g, wr, br, per_row, tm, tiles_per_batch):
    n = x.shape[0]
    row = lambda wd: pl.BlockSpec((tm, wd), lambda t: (t, 0))
    return pl.pallas_call(
        _router_kernel,
        grid=(n // tm,),
        in_specs=[row(D_MODEL), _mod_spec(per_row, tm, tiles_per_batch), _mod_spec(per_row, tm, tiles_per_batch),
                  _full_spec((1, D_MODEL)), _full_spec((D_MODEL, LANE)), _full_spec((1, LANE))],
        out_specs=[row(512), row(LANE)],
        out_shape=[jax.ShapeDtypeStruct((n, 512), U32), jax.ShapeDtypeStruct((n, LANE), F32)],
        compiler_params=_cparams(("arbitrary",)),
        name="router",
    )(x, shift, scale, g, wr, br)


def _routing_offsets(expert_ids, n_tiles_max):
    onehot = (expert_ids[:, None] == jnp.arange(N_EXPERTS, dtype=I32)[None, :]).astype(I32)
    csum = jnp.cumsum(onehot, axis=0)
    counts = csum[-1]
    rank = jnp.sum(onehot * csum, axis=1) - 1
    padded = ((counts + EXPERT_TILE - 1) // EXPERT_TILE) * EXPERT_TILE
    ends = jnp.cumsum(padded)
    starts = ends - padded
    dest = jnp.sum(onehot * starts[None, :], axis=1) + rank
    tile_rows = jnp.arange(n_tiles_max, dtype=I32) * EXPERT_TILE
    tile_expert = jnp.minimum(jnp.sum((tile_rows[:, None] >= ends[None, :]).astype(I32), axis=1), N_EXPERTS - 1)
    n_tiles = (ends[-1] // EXPERT_TILE).reshape(1)
    return dest.astype(I32), tile_expert.astype(I32), n_tiles.astype(I32)


def _dispatch_kernel(dest_ref, hp_ref, xs_in_ref, xs_ref, sem, *, tm):
    del xs_in_ref

    def row_copy(r, k):
        d = dest_ref[0, 0, TOP_EXPERTS * r + k]
        return pltpu.make_async_copy(hp_ref.at[pl.ds(r, 1)], xs_ref.at[pl.ds(d, 1)], sem.at[0])

    def start(r, c):
        for k in range(TOP_EXPERTS):
            row_copy(r, k).start()
        return c

    lax.fori_loop(0, tm, start, 0, unroll=8)
    for k in range(TOP_EXPERTS):
        pltpu.make_async_copy(hp_ref, xs_ref.at[pl.ds(0, tm)], sem.at[0]).wait()


def _dispatch(dest_tiles, hp, xs, tm):
    n = hp.shape[0]
    return pl.pallas_call(
        functools.partial(_dispatch_kernel, tm=tm),
        grid=(n // tm,),
        in_specs=[pl.BlockSpec((1, 1, TOP_EXPERTS * tm), lambda t: (t, 0, 0), memory_space=pltpu.SMEM),
                  pl.BlockSpec((tm, 512), lambda t: (t, 0)),
                  pl.BlockSpec(memory_space=pl.ANY)],
        out_specs=pl.BlockSpec(memory_space=pl.ANY),
        out_shape=jax.ShapeDtypeStruct(xs.shape, U32),
        scratch_shapes=[pltpu.SemaphoreType.DMA((1,))],
        input_output_aliases={2: 0},
        compiler_params=_cparams(("arbitrary",)),
        name="moe_dispatch",
    )(dest_tiles, hp, xs)


def _experts_kernel(te_ref, nt_ref, xs_ref, w1_ref, w3_ref, w2_ref, y_ref):
    del te_ref

    @pl.when(pl.program_id(0) < nt_ref[0])
    def _():
        packed = xs_ref[...]
        hi = lax.bitcast_convert_type(packed & jnp.uint32(0xFFFF0000), F32).astype(BF16)
        lo = lax.bitcast_convert_type(packed << 16, F32).astype(BF16)
        x = jnp.concatenate([hi, lo], axis=1)
        a = jnp.dot(x, w1_ref[0].astype(BF16), preferred_element_type=F32)
        b = jnp.dot(x, w3_ref[0].astype(BF16), preferred_element_type=F32)
        hid = (a / (1.0 + jnp.exp(-a))) * b
        y_ref[...] = jnp.dot(hid.astype(BF16), w2_ref[0].astype(BF16), preferred_element_type=F32)

    @pl.when(pl.program_id(0) >= nt_ref[0])
    def _():
        y_ref[...] = jnp.zeros_like(y_ref)


def _experts(tile_expert, n_tiles, xs, w1, w3, w2):
    n_tiles_max = xs.shape[0] // EXPERT_TILE
    live = lambda t, te, nt: jnp.minimum(t, nt[0] - 1)
    wspec = lambda a, b: pl.BlockSpec((1, a, b), lambda t, te, nt: (te[live(t, te, nt)], 0, 0))
    return pl.pallas_call(
        _experts_kernel,
        grid_spec=pltpu.PrefetchScalarGridSpec(
            num_scalar_prefetch=2,
            grid=(n_tiles_max,),
            in_specs=[pl.BlockSpec((EXPERT_TILE, 512), lambda t, te, nt: (live(t, te, nt), 0)),
                      wspec(D_MODEL, EXPERT_FF), wspec(D_MODEL, EXPERT_FF), wspec(EXPERT_FF, D_MODEL)],
            out_specs=pl.BlockSpec((EXPERT_TILE, D_MODEL), lambda t, te, nt: (t, 0))),
        out_shape=jax.ShapeDtypeStruct((xs.shape[0], D_MODEL), F32),
        compiler_params=_cparams(("arbitrary",)),
        name="moe_experts",
    )(tile_expert, n_tiles, xs, w1, w3, w2)


def _combine_kernel(dest_ref, dest_next_ref, info_ref, x_ref, gmod_ref, y_hbm, o_ref, ybuf, sem, *, tm):
    t = pl.program_id(0)
    nt = pl.num_programs(0)
    slot = t % 2

    def row_copy(dref, slot_, r, k):
        d = dref[0, 0, TOP_EXPERTS * r + k]
        return pltpu.make_async_copy(y_hbm.at[pl.ds(d, 1)], ybuf.at[slot_, pl.ds(k * tm + r, 1)], sem.at[slot_])

    def start_all(dref, slot_):
        def body(r, c):
            for k in range(TOP_EXPERTS):
                row_copy(dref, slot_, r, k).start()
            return c
        lax.fori_loop(0, tm, body, 0, unroll=8)

    @pl.when(t == 0)
    def _():
        start_all(dest_ref, 0)

    @pl.when(t + 1 < nt)
    def _():
        start_all(dest_next_ref, 1 - slot)

    pltpu.make_async_copy(y_hbm.at[pl.ds(0, TOP_EXPERTS * tm)], ybuf.at[slot], sem.at[slot]).wait()
    info = info_ref[...]
    y = info[:, 2:3] * ybuf[slot, 0:tm] + info[:, 3:4] * ybuf[slot, tm:2 * tm]
    o_ref[...] = x_ref[...] + gmod_ref[...] * y


def _combine(dest_tiles, info, x, gmod, y_sorted, per_row, tm, tiles_per_batch):
    n = x.shape[0]
    nt = n // tm
    row = lambda wd: pl.BlockSpec((tm, wd), lambda t: (t, 0))
    dspec = lambda f: pl.BlockSpec((1, 1, TOP_EXPERTS * tm), lambda t: (f(t), 0, 0), memory_space=pltpu.SMEM)
    return pl.pallas_call(
        functools.partial(_combine_kernel, tm=tm),
        grid=(nt,),
        in_specs=[dspec(lambda t: t), dspec(lambda t: jnp.minimum(t + 1, nt - 1)), row(LANE), row(D_MODEL),
                  _mod_spec(per_row, tm, tiles_per_batch), pl.BlockSpec(memory_space=pl.ANY)],
        out_specs=row(D_MODEL),
        out_shape=jax.ShapeDtypeStruct((n, D_MODEL), F32),
        scratch_shapes=[pltpu.VMEM((2, TOP_EXPERTS * tm, D_MODEL), F32), pltpu.SemaphoreType.DMA((2,))],
        compiler_params=_cparams(("arbitrary",)),
        name="moe_combine",
    )(dest_tiles, dest_tiles, info, x, gmod, y_sorted)


def _qk(q, k):
    return lax.dot_general(q, k, (((1,), (1,)), ((), ())), preferred_element_type=F32)


def _causal_bias(t):
    r = lax.broadcasted_iota(I32, (t, t), 0)
    c = lax.broadcasted_iota(I32, (t, t), 1)
    return jnp.where(r >= c, 0.0, NEG)


def _online_softmax_step(m_ref, acc_ref, idx, s, v):
    m_old = m_ref[idx]
    m_new = jnp.maximum(m_old, jnp.max(s, axis=-1, keepdims=True))
    p = jnp.exp(s - m_new)
    acc_ref[idx] = jnp.exp(m_old - m_new) * acc_ref[idx] + jnp.dot(p.astype(BF16), v, preferred_element_type=F32)
    m_ref[idx] = m_new


def _add_tile_bias(s, bias, copies):
    t = bias.shape[0]
    return (s.reshape(copies, t, t) + bias[None]).reshape(copies * t, t)


def _moba_prompt_kernel(qa_ref, kv_ref, kmp_ref, o_ref, q_scr, m_ref, acc_ref, *, nb):
    i = pl.program_id(1)
    blk_rows = MOBA_BLOCK
    rep = MOBA_HEADS // MOBA_KV_HEADS
    rows = rep * blk_rows
    blk = lax.broadcasted_iota(I32, (rows, LANE), 1) - SEL_LANE0
    for g in range(MOBA_KV_HEADS):
        q_st = jnp.concatenate([qa_ref[:, (g * rep + j) * LANE:(g * rep + j + 1) * LANE] for j in range(rep)], axis=0)
        gate = lax.dot_general(q_st.astype(F32), kmp_ref[0, g], (((1,), (1,)), ((), ())),
                               preferred_element_type=F32, precision=HIGHEST)
        valid = (blk >= 0) & (blk < i)
        gate = jnp.where(valid, gate, -jnp.inf)
        rank = jnp.zeros((rows, LANE), I32)
        for mb in range(nb):
            gm = gate[:, SEL_LANE0 + mb:SEL_LANE0 + mb + 1]
            beats = (gm > gate) | ((gm == gate) & (blk > mb))
            rank = rank + beats.astype(I32)
        keep = (valid & (rank < MOBA_TOPK)) | (blk == i)
        bias = jnp.where((blk >= 0) & (blk < nb) & jnp.logical_not(keep), NEG, 0.0)
        q_scr[g] = q_st + bias.astype(BF16)
        m_ref[g] = jnp.full((rows, 1), NEG, F32)
        acc_ref[g] = jnp.zeros((rows, LANE), F32)

    def kv_block(j, g):
        off = pl.multiple_of(j * blk_rows, blk_rows)
        return (kv_ref[pl.ds(off, blk_rows), g * LANE:(g + 1) * LANE],
                kv_ref[pl.ds(off, blk_rows), (MOBA_KV_HEADS + g) * LANE:(MOBA_KV_HEADS + g + 1) * LANE])

    def past(j, c):
        for g in range(MOBA_KV_HEADS):
            kj, vj = kv_block(j, g)
            _online_softmax_step(m_ref, acc_ref, g, _qk(q_scr[g], kj), vj)
        return c

    lax.fori_loop(0, i, past, 0)
    causal = _causal_bias(blk_rows)
    for g in range(MOBA_KV_HEADS):
        kj, vj = kv_block(i, g)
        _online_softmax_step(m_ref, acc_ref, g, _add_tile_bias(_qk(q_scr[g], kj), causal, rep), vj)
        acc = acc_ref[g]
        o = acc[:, 0:MOBA_HEAD_DIM] / acc[:, ONES_LANE:ONES_LANE + 1]
        for j in range(rep):
            hd = g * rep + j
            o_ref[:, hd * 64:(hd + 1) * 64] = o[j * blk_rows:(j + 1) * blk_rows].astype(BF16)


def _moba_prompt(qa, kva, kmp, batch, seq):
    nb = seq // MOBA_BLOCK
    rows = (MOBA_HEADS // MOBA_KV_HEADS) * MOBA_BLOCK
    return pl.pallas_call(
        functools.partial(_moba_prompt_kernel, nb=nb),
        grid=(batch, nb),
        in_specs=[pl.BlockSpec((MOBA_BLOCK, 1024), lambda b, i: (b * nb + i, 0)),
                  pl.BlockSpec((seq, 512), lambda b, i: (b, 0)),
                  pl.BlockSpec((1, MOBA_KV_HEADS, LANE, LANE), lambda b, i: (b, 0, 0, 0))],
        out_specs=pl.BlockSpec((MOBA_BLOCK, 512), lambda b, i: (b * nb + i, 0)),
        out_shape=jax.ShapeDtypeStruct((batch * seq, 512), BF16),
        scratch_shapes=[pltpu.VMEM((MOBA_KV_HEADS, rows, LANE), BF16), pltpu.VMEM((MOBA_KV_HEADS, rows, 1), F32),
                        pltpu.VMEM((MOBA_KV_HEADS, rows, LANE), F32)],
        compiler_params=_cparams(("arbitrary", "arbitrary")),
        name="moba_prompt",
    )(qa, kva, kmp)


def _mla_prompt_kernel(q_ref, k_ref, o_ref, m_ref, acc_ref):
    i = pl.program_id(1)
    tq = TOKEN_TILE
    rows = MLA_HEADS * tq
    m_ref[0] = jnp.full((rows, 1), NEG, F32)
    acc_ref[0] = jnp.zeros((rows, 256), F32)

    def past(j, c):
        kj = k_ref[pl.ds(pl.multiple_of(j * tq, tq), tq), :]
        _online_softmax_step(m_ref, acc_ref, 0, _qk(q_ref[...].reshape(rows, 256), kj), kj)
        return c

    lax.fori_loop(0, i, past, 0)
    kj = k_ref[pl.ds(pl.multiple_of(i * tq, tq), tq), :]
    s = _add_tile_bias(_qk(q_ref[...].reshape(rows, 256), kj), _causal_bias(tq), MLA_HEADS)
    _online_softmax_step(m_ref, acc_ref, 0, s, kj)
    acc = acc_ref[0]
    o = acc[:, 0:MLA_KV_RANK] / acc[:, MLA_ONES_LANE:MLA_ONES_LANE + 1]
    for hd in range(MLA_HEADS):
        o_ref[:, hd * 128:(hd + 1) * 128] = o[hd * tq:(hd + 1) * tq].astype(BF16)


def _mla_prompt(qfull, kfull, batch, seq):
    nq = seq // TOKEN_TILE
    rows = MLA_HEADS * TOKEN_TILE
    return pl.pallas_call(
        _mla_prompt_kernel,
        grid=(batch, nq),
        in_specs=[pl.BlockSpec((MLA_HEADS, TOKEN_TILE, 256), lambda b, i: (0, b * nq + i, 0)),
                  pl.BlockSpec((seq, 256), lambda b, i: (b, 0))],
        out_specs=pl.BlockSpec((TOKEN_TILE, 1024), lambda b, i: (b * nq + i, 0)),
        out_shape=jax.ShapeDtypeStruct((batch * seq, 1024), BF16),
        scratch_shapes=[pltpu.VMEM((1, rows, 1), F32), pltpu.VMEM((1, rows, 256), F32)],
        compiler_params=_cparams(("arbitrary", "arbitrary")),
        name="mla_prompt",
    )(qfull, kfull)


def _lambda_full(lam_ref, lam_init):
    lam = lam_ref[...]
    a = jnp.sum(lam[0:1] * lam[1:2], axis=-1, keepdims=True)
    b = jnp.sum(lam[2:3] * lam[3:4], axis=-1, keepdims=True)
    return jnp.exp(a) - jnp.exp(b) + lam_init


def _diff_prompt_kernel(qa_ref, ka_ref, va_ref, lam_ref, subg_ref, o_ref, q_scr, m_ref, acc_ref, *, lam_init):
    i = pl.program_id(1)
    tq = TOKEN_TILE
    rep = DIFF_HEADS // DIFF_KV_HEADS
    rows = rep * tq
    chains = [(g, c) for g in range(DIFF_KV_HEADS) for c in range(2)]
    for n, (g, c) in enumerate(chains):
        q_scr[n] = jnp.concatenate(
            [qa_ref[:, ((g * rep + j) * 2 + c) * LANE:((g * rep + j) * 2 + c + 1) * LANE] for j in range(rep)], axis=0)
        m_ref[n] = jnp.full((rows, 1), NEG, F32)
        acc_ref[n] = jnp.zeros((rows, 256), F32)

    def kv_block(j, n):
        g, c = chains[n]
        off = pl.multiple_of(j * tq, tq)
        return (ka_ref[pl.ds(off, tq), (g * 2 + c) * LANE:(g * 2 + c + 1) * LANE],
                va_ref[pl.ds(off, tq), g * 256:(g + 1) * 256])

    def past(j, cc):
        for n in range(len(chains)):
            kj, vj = kv_block(j, n)
            _online_softmax_step(m_ref, acc_ref, n, _qk(q_scr[n], kj), vj)
        return cc

    lax.fori_loop(0, i, past, 0)
    causal = _causal_bias(tq)
    lam_full = _lambda_full(lam_ref, lam_init)
    outs = []
    for n in range(len(chains)):
        kj, vj = kv_block(i, n)
        _online_softmax_step(m_ref, acc_ref, n, _add_tile_bias(_qk(q_scr[n], kj), causal, rep), vj)
        acc = acc_ref[n]
        outs.append(acc[:, 0:128] / acc[:, 128:129])
    for g in range(DIFF_KV_HEADS):
        o = outs[2 * g] - lam_full * outs[2 * g + 1]
        o = _rms(o, subg_ref[...]) * (1.0 - lam_init)
        for j in range(rep):
            hd = g * rep + j
            o_ref[:, hd * 128:(hd + 1) * 128] = o[j * tq:(j + 1) * tq].astype(BF16)


def _diff_prompt(qa, ka, va, lam, subg, lam_init, batch, seq):
    nq = seq // TOKEN_TILE
    rows = (DIFF_HEADS // DIFF_KV_HEADS) * TOKEN_TILE
    n_chain = DIFF_KV_HEADS * 2
    return pl.pallas_call(
        functools.partial(_diff_prompt_kernel, lam_init=lam_init),
        grid=(batch, nq),
        in_specs=[pl.BlockSpec((TOKEN_TILE, 2048), lambda b, i: (b * nq + i, 0)),
                  pl.BlockSpec((seq, 512), lambda b, i: (b, 0)),
                  pl.BlockSpec((seq, 512), lambda b, i: (b, 0)),
                  pl.BlockSpec((4, 64), lambda b, i: (0, 0)),
                  pl.BlockSpec((1, 128), lambda b, i: (0, 0))],
        out_specs=pl.BlockSpec((TOKEN_TILE, 1024), lambda b, i: (b * nq + i, 0)),
        out_shape=jax.ShapeDtypeStruct((batch * seq, 1024), BF16),
        scratch_shapes=[pltpu.VMEM((n_chain, rows, LANE), BF16), pltpu.VMEM((n_chain, rows, 1), F32),
                        pltpu.VMEM((n_chain, rows, 256), F32)],
        compiler_params=_cparams(("arbitrary", "arbitrary")),
        name="diff_prompt",
    )(qa, ka, va, lam, subg)


def _page_copy(src_hbm, page, dst, sem):
    return pltpu.make_async_copy(src_hbm.at[page], dst, sem)


def _paged_pipeline(pt_ref, n_pages, layer_off, streams, sem):
    s = pl.program_id(0)
    ns = pl.num_programs(0)
    slot = s % 2

    def start(seq, slot_):
        for p in range(n_pages):
            page = pt_ref[seq * n_pages + p] + layer_off
            for hbm, buf, dst_fn in streams:
                _page_copy(hbm, page, dst_fn(buf, slot_, p), sem.at[slot_]).start()

    @pl.when(s == 0)
    def _():
        start(0, 0)

    @pl.when(s + 1 < ns)
    def _():
        start(s + 1, 1 - slot)

    for p in range(n_pages):
        for hbm, buf, dst_fn in streams:
            _page_copy(hbm, 0, dst_fn(buf, slot, p), sem.at[slot]).wait()
    return slot


def _lane_window(buf, slot, p):
    return buf.at[slot, :, pl.ds(p * LANE, LANE)]


def _row_window(rows):
    return lambda buf, slot, p: buf.at[slot, pl.ds(p * rows, rows), :]


def _moba_decode_kernel(pt_ref, q_ref, knew_ref, vnew_ref, k_hbm, v_hbm, o_ref, kbuf, vbuf, sem, *,
                        n_pages, layer_off, past_len):
    slot = _paged_pipeline(pt_ref, n_pages, layer_off, [(k_hbm, kbuf, _lane_window), (v_hbm, vbuf, _lane_window)],
                           sem)
    nblk = past_len // MOBA_BLOCK
    q = q_ref[0]
    kt = kbuf[slot].astype(BF16)
    s_raw = jnp.dot(q, kt, preferred_element_type=F32)
    lane = lax.broadcasted_iota(I32, (MOBA_HEADS, LANE), 1)
    gate = jnp.full((MOBA_HEADS, LANE), -jnp.inf, F32)
    for b in range(nblk):
        gs = jnp.sum(s_raw[:, b * MOBA_BLOCK:(b + 1) * MOBA_BLOCK], axis=-1, keepdims=True)
        gate = jnp.where(lane == b, gs, gate)
    rank = jnp.zeros((MOBA_HEADS, LANE), I32)
    for mb in range(nblk):
        gm = gate[:, mb:mb + 1]
        beats = (gm > gate) | ((gm == gate) & (lane > mb))
        rank = rank + beats.astype(I32)
    sel = (lane < nblk) & (rank < MOBA_TOPK)
    selmask = jnp.concatenate(
        [jnp.broadcast_to(jnp.sum(jnp.where((lane == b) & sel, 1.0, 0.0), axis=-1, keepdims=True) > 0.5,
                          (MOBA_HEADS, MOBA_BLOCK)) for b in range(nblk)], axis=1)
    slope = _slope_column(_alibi_slopes(MOBA_HEADS))
    pos = lax.broadcasted_iota(I32, (MOBA_HEADS, past_len), 1)
    dist = (past_len - pos).astype(F32)
    s = jnp.where(selmask, s_raw - slope * dist, NEG)
    s_own = jnp.sum(q.astype(F32) * knew_ref[0], axis=-1, keepdims=True)
    m = jnp.maximum(jnp.max(s, axis=-1, keepdims=True), s_own)
    p = jnp.exp(s - m)
    p_own = jnp.exp(s_own - m)
    l = jnp.sum(p, axis=-1, keepdims=True) + p_own
    vt = vbuf[slot].astype(BF16)
    acc = _qk(p.astype(BF16), vt) + p_own * vnew_ref[0]
    o_ref[0] = acc / l


def _moba_decode(pt_flat, q_bd, knew, vnew, k_pages, v_pages, layer, n_pool, n_seq, n_pages):
    past_len = n_pages * LANE
    kern = functools.partial(_moba_decode_kernel, n_pages=n_pages, layer_off=layer * n_pool, past_len=past_len)
    return pl.pallas_call(
        kern,
        grid_spec=pltpu.PrefetchScalarGridSpec(
            num_scalar_prefetch=1,
            grid=(n_seq,),
            in_specs=[pl.BlockSpec((1, 8, 128), lambda s, pt: (s, 0, 0)),
                      pl.BlockSpec((1, 1, 128), lambda s, pt: (s, 0, 0)),
                      pl.BlockSpec((1, 1, 128), lambda s, pt: (s, 0, 0)),
                      pl.BlockSpec(memory_space=pl.ANY), pl.BlockSpec(memory_space=pl.ANY)],
            out_specs=pl.BlockSpec((1, 8, 128), lambda s, pt: (s, 0, 0)),
            scratch_shapes=[pltpu.VMEM((2, 128, past_len), F32), pltpu.VMEM((2, 128, past_len), F32),
                            pltpu.SemaphoreType.DMA((2,))]),
        out_shape=jax.ShapeDtypeStruct((n_seq, 8, 128), F32),
        compiler_params=_cparams(("arbitrary",)),
        name="moba_decode",
    )(pt_flat, q_bd, knew, vnew, k_pages, v_pages)


def _mla_decode_kernel(pt_ref, qlat_ref, qpe_ref, cnew_ref, pnew_ref, c_hbm, p_hbm, o_ref, cbuf, pbuf, sem, *,
                       n_pages, layer_off):
    slot = _paged_pipeline(pt_ref, n_pages, layer_off,
                           [(c_hbm, cbuf, _row_window(LANE)), (p_hbm, pbuf, _lane_window)], sem)
    qlat = qlat_ref[0]
    qpe = qpe_ref[0]
    ckv = cbuf[slot].astype(BF16)
    kpet = pbuf[slot].astype(BF16)
    s = _qk(qlat, ckv) + jnp.dot(qpe, kpet, preferred_element_type=F32)
    s_own = (jnp.sum(qlat.astype(F32) * cnew_ref[0], axis=-1, keepdims=True)
             + jnp.sum(qpe.astype(F32) * pnew_ref[0], axis=-1, keepdims=True))
    m = jnp.maximum(jnp.max(s, axis=-1, keepdims=True), s_own)
    p = jnp.exp(s - m)
    p_own = jnp.exp(s_own - m)
    l = jnp.sum(p, axis=-1, keepdims=True) + p_own
    acc = jnp.dot(p.astype(BF16), ckv, preferred_element_type=F32) + p_own * cnew_ref[0]
    o_ref[0] = (acc / l).astype(BF16)


def _mla_decode(pt_flat, qlat, qpe, cnew, pnew, c_pages, p_pages, layer, n_pool, n_seq, n_pages):
    past_len = n_pages * LANE
    kern = functools.partial(_mla_decode_kernel, n_pages=n_pages, layer_off=layer * n_pool)
    return pl.pallas_call(
        kern,
        grid_spec=pltpu.PrefetchScalarGridSpec(
            num_scalar_prefetch=1,
            grid=(n_seq,),
            in_specs=[pl.BlockSpec((1, 8, 128), lambda s, pt: (s, 0, 0)),
                      pl.BlockSpec((1, 8, 32), lambda s, pt: (s, 0, 0)),
                      pl.BlockSpec((1, 1, 128), lambda s, pt: (s, 0, 0)),
                      pl.BlockSpec((1, 1, 32), lambda s, pt: (s, 0, 0)),
                      pl.BlockSpec(memory_space=pl.ANY), pl.BlockSpec(memory_space=pl.ANY)],
            out_specs=pl.BlockSpec((1, 8, 128), lambda s, pt: (s, 0, 0)),
            scratch_shapes=[pltpu.VMEM((2, past_len, 128), F32), pltpu.VMEM((2, MLA_ROPE_DIM, past_len), F32),
                            pltpu.SemaphoreType.DMA((2,))]),
        out_shape=jax.ShapeDtypeStruct((n_seq, 8, 128), BF16),
        compiler_params=_cparams(("arbitrary",)),
        name="mla_decode",
    )(pt_flat, qlat, qpe, cnew, pnew, c_pages, p_pages)


def _diff_decode_kernel(pt_ref, q_ref, knew_ref, vnew_ref, lam_ref, subg_ref, k_hbm, v_hbm, o_ref,
                        kbuf, vbuf, sem, m_ref, l_ref, acc_ref, *, n_pages, pages_per_step, layer_off, lam_init):
    st = pl.program_id(0)
    halves = n_pages // pages_per_step
    half = st % halves
    slot = _paged_pipeline(pt_ref, pages_per_step, layer_off,
                           [(k_hbm, kbuf, _lane_window), (v_hbm, vbuf, _row_window(2 * LANE))], sem)
    chunk = pages_per_step * LANE
    q = q_ref[0]
    kt = kbuf[slot].astype(BF16)
    head_slopes = _alibi_slopes(DIFF_HEADS)
    slope = _slope_column([head_slopes[(r // 8) * 4 + (r % 4)] for r in range(16)])
    pos = lax.broadcasted_iota(I32, (16, chunk), 1) + half * chunk
    dist = (n_pages * LANE - pos).astype(F32)
    s = jnp.dot(q, kt, preferred_element_type=F32) - slope * dist

    @pl.when(half == 0)
    def _():
        m_ref[...] = jnp.full_like(m_ref, NEG)
        l_ref[...] = jnp.zeros_like(l_ref)
        acc_ref[...] = jnp.zeros_like(acc_ref)

    m_old = m_ref[...]
    m_new = jnp.maximum(m_old, jnp.max(s, axis=-1, keepdims=True))
    alpha = jnp.exp(m_old - m_new)
    p = jnp.exp(s - m_new)
    l_ref[...] = alpha * l_ref[...] + jnp.sum(p, axis=-1, keepdims=True)
    pb = p.astype(BF16)
    pv = []
    for g in range(DIFF_KV_HEADS):
        vg = vbuf[slot, pl.ds(g, chunk, stride=2), :].astype(BF16)
        pv.append(jnp.dot(pb[g * 8:(g + 1) * 8], vg, preferred_element_type=F32))
    acc_ref[...] = alpha * acc_ref[...] + jnp.concatenate(pv, axis=0)
    m_ref[...] = m_new

    @pl.when(half == halves - 1)
    def _():
        s_own = jnp.sum(q.astype(F32) * knew_ref[0], axis=-1, keepdims=True)
        m_o = m_ref[...]
        m_f = jnp.maximum(m_o, s_own)
        a = jnp.exp(m_o - m_f)
        p_own = jnp.exp(s_own - m_f)
        l = a * l_ref[...] + p_own
        vnew = vnew_ref[0]
        vrows = jnp.concatenate([jnp.broadcast_to(vnew[:, g * 128:(g + 1) * 128], (8, 128))
                                 for g in range(DIFF_KV_HEADS)], axis=0)
        o = (a * acc_ref[...] + p_own * vrows) / l
        lam_full = _lambda_full(lam_ref, lam_init)
        res = []
        for g in range(DIFF_KV_HEADS):
            o1 = o[g * 8:g * 8 + 4]
            o2 = o[g * 8 + 4:g * 8 + 8]
            res.append(_rms(o1 - lam_full * o2, subg_ref[...]) * (1.0 - lam_init))
        o_ref[0] = jnp.concatenate(res, axis=0).astype(BF16)


def _diff_decode(pt_flat, q_bd, knew, vnew, lam, subg, k_pages, v_pages, layer, n_pool, n_seq, n_pages, lam_init):
    halves = 2
    pps = n_pages // halves
    chunk = pps * LANE
    kern = functools.partial(_diff_decode_kernel, n_pages=n_pages, pages_per_step=pps, layer_off=layer * n_pool,
                             lam_init=lam_init)
    return pl.pallas_call(
        kern,
        grid_spec=pltpu.PrefetchScalarGridSpec(
            num_scalar_prefetch=1,
            grid=(n_seq * halves,),
            in_specs=[pl.BlockSpec((1, 16, 256), lambda s, pt: (s // halves, 0, 0)),
                      pl.BlockSpec((1, 1, 256), lambda s, pt: (s // halves, 0, 0)),
                      pl.BlockSpec((1, 1, 256), lambda s, pt: (s // halves, 0, 0)),
                      pl.BlockSpec((4, 64), lambda s, pt: (0, 0)),
                      pl.BlockSpec((1, 128), lambda s, pt: (0, 0)),
                      pl.BlockSpec(memory_space=pl.ANY), pl.BlockSpec(memory_space=pl.ANY)],
            out_specs=pl.BlockSpec((1, 8, 128), lambda s, pt: (s // halves, 0, 0)),
            scratch_shapes=[pltpu.VMEM((2, 256, chunk), F32), pltpu.VMEM((2, 2 * chunk, 128), F32),
                            pltpu.SemaphoreType.DMA((2,)),
                            pltpu.VMEM((16, 1), F32), pltpu.VMEM((16, 1), F32), pltpu.VMEM((16, 128), F32)]),
        out_shape=jax.ShapeDtypeStruct((n_seq, 8, 128), BF16),
        compiler_params=_cparams(("arbitrary",)),
        name="diff_decode",
    )(pt_flat, q_bd, knew, vnew, lam, subg, k_pages, v_pages)


def _rope_tables(pos):
    half = MLA_ROPE_DIM // 2
    inv = ROPE_BASE ** (-jnp.arange(half, dtype=F32) / half)
    ang = pos.astype(F32)[:, None] * inv[None, :]
    pad = jnp.zeros((pos.shape[0], LANE - MLA_ROPE_DIM), F32)
    cos, sin = jnp.cos(ang), jnp.sin(ang)
    return jnp.concatenate([cos, cos, pad], axis=1), jnp.concatenate([sin, sin, pad], axis=1)


def _swap_halves(w):
    half = w.shape[-1] // 2
    return jnp.concatenate([-w[..., half:], w[..., :half]], axis=-1)


def _pad_cols(w, width):
    return jnp.pad(w, [(0, 0)] * (w.ndim - 1) + [(0, width - w.shape[-1])])


def _pad_blocks(w, block, width):
    rows = w.shape[0]
    return _pad_cols(w.reshape(rows, -1, block), width).reshape(rows, -1)


def _ab_weights(w_in, w_uq, w_uk, w_uv):
    wq, wk, wv = w_in[:, 0:512], w_in[:, 512:640], w_in[:, 640:768]
    kpe_w = w_in[:, 1152:1184]
    w1 = jnp.concatenate([_pad_blocks(wq, 64, LANE), wk, wv, _pad_blocks(wk, 64, LANE), _pad_blocks(wv, 64, LANE),
                          w_in[:, 768:1152], _pad_cols(kpe_w, LANE), _pad_cols(_swap_halves(kpe_w), LANE)], axis=1)
    uq = w_uq.reshape(MLA_Q_RANK, MLA_HEADS, MLA_NOPE_DIM + MLA_ROPE_DIM)
    nope = uq[:, :, :MLA_NOPE_DIM].reshape(MLA_Q_RANK, MLA_HEADS * MLA_NOPE_DIM)
    pe = uq[:, :, MLA_NOPE_DIM:]
    pe_pad = _pad_cols(pe, LANE).reshape(MLA_Q_RANK, MLA_HEADS * LANE)
    pes_pad = _pad_cols(_swap_halves(pe), LANE).reshape(MLA_Q_RANK, MLA_HEADS * LANE)
    wuq = jnp.concatenate([nope, pe_pad, pes_pad], axis=1)
    wukt = jnp.transpose(w_uk, (1, 2, 0))
    eye = jnp.eye(MLA_HEADS, dtype=F32)
    wuv_bd = (jnp.transpose(w_uv, (1, 0, 2))[:, :, None, :] * eye[:, None, :, None]).reshape(
        MLA_HEADS * MLA_KV_RANK, MLA_HEADS * MLA_V_DIM)
    return w1.astype(BF16), wuq.astype(BF16), wukt.astype(BF16), wuv_bd.astype(BF16)


def _c_weights(w_in):
    wq, wk, wv = w_in[:, 0:1024], w_in[:, 1024:1280], w_in[:, 1280:1536]
    w = jnp.concatenate([_pad_blocks(wq, 64, LANE), wk, wv, _pad_blocks(wk, 64, LANE), _pad_blocks(wv, 128, 256)],
                        axis=1)
    return w.astype(BF16)


def _lane_table(rows, width, entries):
    lane = jnp.arange(width, dtype=I32)[None, :]
    tab = jnp.zeros((rows, width), F32)
    for ln, val in entries.items():
        tab = jnp.where(lane == ln, jnp.broadcast_to(jnp.asarray(val, F32).reshape(-1, 1), (rows, 1)), tab)
    return tab


def _query_consts(slopes, blocks_per_head):
    ent = {}
    for h, s in enumerate(slopes):
        for c in range(blocks_per_head):
            base = (h * blocks_per_head + c) * LANE
            ent[base + POS_HI_LANE] = s * 256.0
            ent[base + POS_LO_LANE] = s
    return _lane_table(1, len(slopes) * blocks_per_head * LANE, ent)


def _key_tables(seq):
    pos = jnp.arange(seq, dtype=I32)
    hi = (pos // 256).astype(F32)
    lo = (pos % 256).astype(F32)
    nb = seq // MOBA_BLOCK
    ent = {}
    for g in range(MOBA_KV_HEADS):
        ent[g * LANE + POS_HI_LANE] = hi
        ent[g * LANE + POS_LO_LANE] = lo
        for n in range(nb):
            ent[g * LANE + SEL_LANE0 + n] = (pos // MOBA_BLOCK == n).astype(F32)
        ent[(MOBA_KV_HEADS + g) * LANE + ONES_LANE] = jnp.ones((seq,), F32)
    ktab_ab = _lane_table(seq, 2 * MOBA_KV_HEADS * LANE, ent)
    ent = {}
    for gc in range(2 * DIFF_KV_HEADS):
        ent[gc * LANE + POS_HI_LANE] = hi
        ent[gc * LANE + POS_LO_LANE] = lo
    for g in range(DIFF_KV_HEADS):
        ent[512 + g * 256 + 128] = jnp.ones((seq,), F32)
    ktab_c = _lane_table(seq, 1024, ent)
    return ktab_ab, ktab_c


def _block_diag_rows(q, blocks, width):
    n, rows, _ = q.shape
    nblk = max(blocks) + 1
    sel = jnp.asarray([[1.0 if blocks[r] == b else 0.0 for b in range(nblk)] for r in range(rows)], q.dtype)
    return (q[:, :, None, :] * sel[None, :, :, None]).reshape(n, rows, nblk * width)


def kernel(x_prompt, x_sample, c_prompt, c_sample, cache_moba_k, cache_moba_v, cache_mla_ckv, cache_mla_kpe,
           cache_diff_k, cache_diff_v, page_table, mod_w, mod_b, norm_mix, norm_ffn, ab_w_in, mla_q_norm,
           mla_kv_norm, mla_w_uq, mla_w_uk, mla_w_uv, ab_w_out, c_w_in, diff_lambda, diff_subln, c_w_out,
           moe_w_group, moe_b_group, moe_w_expert, moe_b_expert, moe_w1, moe_w3, moe_w2, final_norm):
    batch, seq, d = x_prompt.shape
    n_seq = x_sample.shape[0]
    depth = mod_w.shape[0]
    n_pool = cache_moba_k.shape[1]
    n_pages = page_table.shape[1]
    past_len = n_pages * cache_moba_k.shape[2]
    nb = seq // MOBA_BLOCK
    assert d == D_MODEL and x_sample.shape[1] == 1 and cache_moba_k.shape[2] == LANE
    assert seq % MOBA_BLOCK == 0 and past_len % MOBA_BLOCK == 0 and n_seq % 8 == 0
    assert SEL_LANE0 + nb <= LANE and seq <= 256 * 256
    n_p = batch * seq
    tpb = seq // TOKEN_TILE

    xp = x_prompt.reshape(n_p, d)
    xs = x_sample.reshape(n_seq, d)
    mod = _modulation(jnp.concatenate([c_prompt, c_sample], axis=0), mod_w, mod_b)
    mod = mod.reshape(depth, batch + n_seq, 6, d)
    pt_flat = page_table.reshape(-1)

    n_all = cache_moba_k.shape[0] * n_pool
    mk_pages = jnp.transpose(cache_moba_k, (0, 1, 3, 4, 2)).reshape(n_all, 128, LANE)
    mv_pages = jnp.transpose(cache_moba_v, (0, 1, 3, 4, 2)).reshape(n_all, 128, LANE)
    ckv_pages = cache_mla_ckv.reshape(n_all, LANE, 128)
    kpe_pages = jnp.transpose(cache_mla_kpe, (0, 1, 3, 2)).reshape(n_all, MLA_ROPE_DIM, LANE)
    n_all_c = cache_diff_k.shape[0] * n_pool
    dk_pages = jnp.transpose(cache_diff_k, (0, 1, 3, 4, 5, 2)).reshape(n_all_c, 256, LANE)
    dv_pages = cache_diff_v.reshape(n_all_c, 2 * LANE, 128)

    cos_p, sin_p = _rope_tables(jnp.arange(seq, dtype=I32))
    cos_s, sin_s = _rope_tables(jnp.full((1,), past_len, I32))
    ktab_ab, ktab_c = _key_tables(seq)
    ktab_ab_s = jnp.zeros((1, ktab_ab.shape[1]), F32)
    ktab_c_s = jnp.zeros((1, ktab_c.shape[1]), F32)
    qconst_ab = _query_consts(_alibi_slopes(MOBA_HEADS), 1)
    qconst_c = _query_consts(_alibi_slopes(DIFF_HEADS), 2)
    kones = _lane_table(1, LANE, {MLA_ONES_LANE - MLA_KV_RANK: 1.0})

    n_exp_all = depth * N_EXPERTS
    w1_all = moe_w1.reshape(n_exp_all, d, EXPERT_FF)
    w3_all = moe_w3.reshape(n_exp_all, d, EXPERT_FF)
    w2_all = moe_w2.reshape(n_exp_all, EXPERT_FF, d)
    n_slots = TOP_EXPERTS * (n_p + n_seq)
    n_tiles_max = -(-n_slots // EXPERT_TILE) + N_EXPERTS

    moba_blocks = [h // (MOBA_HEADS // MOBA_KV_HEADS) for h in range(MOBA_HEADS)]
    diff_rows = [(g, c, j) for g in range(DIFF_KV_HEADS) for c in range(2) for j in range(4)]

    new_p = {k: [] for k in ("mk", "mv", "ckv", "kpe", "dk", "dv")}
    new_s = {k: [] for k in ("mk", "mv", "ckv", "kpe", "dk", "dv")}
    for l in range(depth):
        i = l // 2
        mp = [mod[l, :batch, k].reshape(batch, 1, d) for k in range(6)]
        ms = [mod[l, batch:, k] for k in range(6)]
        g_mix = norm_mix[l].reshape(1, d)
        if l % 2 == 0:
            w1, wuq, wukt, wuv_bd = _ab_weights(ab_w_in[i], mla_w_uq[i], mla_w_uk[i], mla_w_uv[i])
            wout = ab_w_out[i].astype(BF16)
            qg = mla_q_norm[i].reshape(1, -1)
            kvg = mla_kv_norm[i].reshape(1, -1)
            qa, mk, mv, kva, kmean, qfull, ckv, kpe, kfull = _inproj_ab(
                xp, mp[0], mp[1], g_mix, w1, qg, kvg, wuq, wukt, cos_p, sin_p, qconst_ab, ktab_ab, kones,
                False, TOKEN_TILE, tpb)
            km = kmean.reshape(batch, tpb, MOBA_KV_HEADS, 64).transpose(0, 2, 1, 3)
            kmp = jnp.pad(km, ((0, 0), (0, 0), (SEL_LANE0, LANE - SEL_LANE0 - tpb), (0, LANE - 64)))
            oa = _moba_prompt(qa, kva, kmp, batch, seq)
            olat = _mla_prompt(qfull, kfull, batch, seq)
            xp = _outproj_ab(xp, mp[2], oa, olat, wuv_bd, wout, False, TOKEN_TILE, tpb)
            new_p["mk"].append(mk); new_p["mv"].append(mv); new_p["ckv"].append(ckv); new_p["kpe"].append(kpe)
            qa, mk, mv, kva, kmean, qfull, ckv, kpe, kfull = _inproj_ab(
                xs, ms[0], ms[1], g_mix, w1, qg, kvg, wuq, wukt, cos_s, sin_s, qconst_ab, ktab_ab_s, kones,
                True, n_seq, 1)
            q_bd = _block_diag_rows(qa.reshape(n_seq, MOBA_HEADS, LANE)[:, :, :64], moba_blocks, 64)
            o_raw = _moba_decode(pt_flat, q_bd, mk.reshape(n_seq, 1, 128), mv.reshape(n_seq, 1, 128),
                                 mk_pages, mv_pages, i, n_pool, n_seq, n_pages)
            o_raw = o_raw.reshape(n_seq, MOBA_KV_HEADS, 4, MOBA_KV_HEADS, 64)
            oa = jnp.concatenate([o_raw[:, g, :, g, :] for g in range(MOBA_KV_HEADS)], axis=1)
            oa = oa.reshape(n_seq, 512).astype(BF16)
            qf3 = jnp.transpose(qfull, (1, 0, 2))
            olat = _mla_decode(pt_flat, qf3[:, :, :128], qf3[:, :, 128:128 + MLA_ROPE_DIM],
                               ckv.reshape(n_seq, 1, 128), kpe.reshape(n_seq, 1, MLA_ROPE_DIM),
                               ckv_pages, kpe_pages, i, n_pool, n_seq, n_pages)
            xs = _outproj_ab(xs, ms[2], oa, olat.reshape(n_seq, 1024), wuv_bd, wout, True, n_seq, 1)
            new_s["mk"].append(mk); new_s["mv"].append(mv); new_s["ckv"].append(ckv); new_s["kpe"].append(kpe)
        else:
            lam_init = 0.8 - 0.6 * math.exp(-0.3 * l)
            wc = _c_weights(c_w_in[i])
            wout = c_w_out[i].astype(BF16)
            lam = diff_lambda[i]
            subg = diff_subln[i].reshape(1, -1)
            qa, k, v, ka, va = _inproj_c(xp, mp[0], mp[1], g_mix, wc, qconst_c, ktab_c, False, TOKEN_TILE, tpb)
            o = _diff_prompt(qa, ka, va, lam, subg, lam_init, batch, seq)
            xp = _outproj_c(xp, mp[2], o, wout, False, TOKEN_TILE, tpb)
            new_p["dk"].append(k); new_p["dv"].append(v)
            qa, k, v, ka, va = _inproj_c(xs, ms[0], ms[1], g_mix, wc, qconst_c, ktab_c_s, True, n_seq, 1)
            q4 = qa.reshape(n_seq, DIFF_HEADS, 2, LANE)[:, :, :, :64]
            q_rows = jnp.stack([q4[:, g * 4 + j, c, :] for (g, c, j) in diff_rows], axis=1)
            q_bd = _block_diag_rows(q_rows, [g * 2 + c for (g, c, j) in diff_rows], 64)
            o = _diff_decode(pt_flat, q_bd, k.reshape(n_seq, 1, 256), v.reshape(n_seq, 1, 256), lam, subg,
                             dk_pages, dv_pages, i, n_pool, n_seq, n_pages, lam_init)
            xs = _outproj_c(xs, ms[2], o.reshape(n_seq, 1024), wout, True, n_seq, 1)
            new_s["dk"].append(k); new_s["dv"].append(v)
        wr = jnp.concatenate([moe_w_expert[l], moe_w_group[l],
                              jnp.zeros((d, LANE - N_EXPERTS - N_GROUPS), F32)], axis=1)
        br = jnp.concatenate([moe_b_expert[l], moe_b_group[l],
                              jnp.zeros((LANE - N_EXPERTS - N_GROUPS,), F32)]).reshape(1, LANE)
        g_ffn = norm_ffn[l].reshape(1, d)
        hp_p, info_p = _router(xp, mp[3], mp[4], g_ffn, wr, br, False, TOKEN_TILE, tpb)
        hp_s, info_s = _router(xs, ms[3], ms[4], g_ffn, wr, br, True, n_seq, 1)
        ids = jnp.concatenate([info_p[:, 0:TOP_EXPERTS], info_s[:, 0:TOP_EXPERTS]], axis=0).astype(I32).reshape(-1)
        dest, tile_expert, n_tiles = _routing_offsets(ids, n_tiles_max)
        dest_p = dest[:TOP_EXPERTS * n_p].reshape(n_p // TOKEN_TILE, 1, TOP_EXPERTS * TOKEN_TILE)
        dest_s = dest[TOP_EXPERTS * n_p:].reshape(1, 1, TOP_EXPERTS * n_seq)
        x_sorted = jnp.zeros((n_tiles_max * EXPERT_TILE, 512), U32)
        x_sorted = _dispatch(dest_p, hp_p, x_sorted, TOKEN_TILE)
        x_sorted = _dispatch(dest_s, hp_s, x_sorted, n_seq)
        y_sorted = _experts(tile_expert + l * N_EXPERTS, n_tiles, x_sorted, w1_all, w3_all, w2_all)
        xp = _combine(dest_p, info_p, xp, mp[5], y_sorted, False, TOKEN_TILE, tpb)
        xs = _combine(dest_s, info_s, xs, ms[5], y_sorted, True, n_seq, 1)

    fg = final_norm.reshape(1, d)
    y_prompt = _final_norm(xp, fg, TOKEN_TILE).reshape(batch, seq, d)
    y_sample = _final_norm(xs, fg, n_seq).reshape(n_seq, 1, d)

    def stack(lst, shape):
        return jnp.stack([a.reshape(shape) for a in lst])

    outs_p = (stack(new_p["mk"], (batch, seq, MOBA_KV_HEADS, MOBA_HEAD_DIM)),
              stack(new_p["mv"], (batch, seq, MOBA_KV_HEADS, MOBA_HEAD_DIM)),
              stack(new_p["ckv"], (batch, seq, MLA_KV_RANK)),
              stack(new_p["kpe"], (batch, seq, MLA_ROPE_DIM)),
              stack(new_p["dk"], (batch, seq, DIFF_KV_HEADS, 2, DIFF_HEAD_DIM)),
              stack(new_p["dv"], (batch, seq, DIFF_KV_HEADS, 2 * DIFF_HEAD_DIM)))
    outs_s = (stack(new_s["mk"], (n_seq, 1, MOBA_KV_HEADS, MOBA_HEAD_DIM)),
              stack(new_s["mv"], (n_seq, 1, MOBA_KV_HEADS, MOBA_HEAD_DIM)),
              stack(new_s["ckv"], (n_seq, 1, MLA_KV_RANK)),
              stack(new_s["kpe"], (n_seq, 1, MLA_ROPE_DIM)),
              stack(new_s["dk"], (n_seq, 1, DIFF_KV_HEADS, 2, DIFF_HEAD_DIM)),
              stack(new_s["dv"], (n_seq, 1, DIFF_KV_HEADS, 2 * DIFF_HEAD_DIM)))
    return (y_prompt, y_sample) + outs_p + outs_s
```

```python
import functools
import math

import jax
import jax.numpy as jnp
from jax import lax
from jax.experimental import pallas as pl
from jax.experimental.pallas import tpu as pltpu

F32 = jnp.float32
BF16 = jnp.bfloat16
U32 = jnp.uint32
I32 = jnp.int32
HIGHEST = lax.Precision.HIGHEST

D_MODEL = 1024
MOBA_HEADS, MOBA_KV_HEADS, MOBA_HEAD_DIM, MOBA_BLOCK, MOBA_TOPK = 8, 2, 64, 256, 3
MLA_HEADS, MLA_Q_RANK, MLA_KV_RANK, MLA_NOPE_DIM, MLA_ROPE_DIM, MLA_V_DIM = 8, 256, 128, 64, 32, 64
ROPE_BASE = 10000.0
DIFF_HEADS, DIFF_KV_HEADS, DIFF_HEAD_DIM = 8, 2, 64
N_GROUPS, EXPERTS_PER_GROUP, EXPERT_FF, TOP_EXPERTS = 4, 8, 256, 2
N_EXPERTS = N_GROUPS * EXPERTS_PER_GROUP
EPS = 1e-6
NEG = -1e30

LANE = 128
VMEM_LIMIT = 56 * 1024 * 1024
TOKEN_TILE = 256
EXPERT_TILE = 256
SEL_LANE0, SEL_LANES = 64, 16
POS_HI_LANE, POS_LO_LANE = 80, 81
ONES_LANE = 64
MOBA_V_ROWS = 80
DIFF_V_ROWS = 144
MLA_V_ROWS = 176
MLA_ONES_LANE = MLA_KV_RANK + MLA_ROPE_DIM
AB_IN_EXT = 1024 + 256 + 512 + 256 + 128 + 128 + 128
UQ_EXT = 512 + 1024 + 1024
C_IN_EXT = 2048 + 256 + 256 + 512 + 512


def _cparams(sem):
    return pltpu.CompilerParams(dimension_semantics=sem, vmem_limit_bytes=VMEM_LIMIT)


def _alibi_slopes(n):
    return [2.0 ** (-8.0 * (i + 1) / n) for i in range(n)]


def _slope_column(values):
    row = lax.broadcasted_iota(I32, (len(values), 1), 0)
    col = jnp.zeros((len(values), 1), F32)
    for r, v in enumerate(values):
        col = jnp.where(row == r, v, col)
    return col


def _rms(x, g):
    return x * lax.rsqrt(jnp.mean(x * x, axis=-1, keepdims=True) + EPS) * g


def _ada(x, g, shift, scale):
    return _rms(x, g) * (1.0 + scale) + shift


def _mod_kernel(c_ref, w_ref, b_ref, o_ref):
    c = c_ref[...]
    a = c / (1.0 + jnp.exp(-c))
    o_ref[0] = jnp.dot(a, w_ref[0], preferred_element_type=F32, precision=HIGHEST) + b_ref[0]


def _modulation(c_all, mod_w, mod_b):
    depth, d, n6 = mod_w.shape
    rows = c_all.shape[0]
    tn = 512
    return pl.pallas_call(
        _mod_kernel,
        grid=(depth, n6 // tn),
        in_specs=[
            pl.BlockSpec((rows, d), lambda l, j: (0, 0)),
            pl.BlockSpec((1, d, tn), lambda l, j: (l, 0, j)),
            pl.BlockSpec((1, 1, tn), lambda l, j: (l, 0, j)),
        ],
        out_specs=pl.BlockSpec((1, rows, tn), lambda l, j: (l, 0, j)),
        out_shape=jax.ShapeDtypeStruct((depth, rows, n6), F32),
        compiler_params=_cparams(("arbitrary", "arbitrary")),
        name="modulation",
    )(c_all, mod_w, mod_b.reshape(depth, 1, n6))


def _mod_spec(per_row, tm, tiles_per_batch):
    if per_row:
        return pl.BlockSpec((tm, D_MODEL), lambda t: (t, 0))
    return pl.BlockSpec((None, 1, D_MODEL), lambda t: (t // tiles_per_batch, 0, 0))


def _pos_spec(per_row, tm, width, tiles_per_batch):
    if per_row:
        return pl.BlockSpec((1, width), lambda t: (0, 0))
    return pl.BlockSpec((tm, width), lambda t: (t % tiles_per_batch, 0))


def _full_spec(shape):
    nd = len(shape)
    return pl.BlockSpec(shape, lambda t: (0,) * nd)


def _inproj_ab_kernel(x_ref, sh_ref, sc_ref, g_ref, w1_ref, qg_ref, kvg_ref, wuq_ref, wukt_ref, cos_ref, sin_ref,
                      qconst_ref, ktab_ref, kones_ref,
                      qt_ref, mk_ref, mv_ref, ka_ref, vt_ref, kmean_ref, qft_ref, ckv_ref, kpe_ref, kfull_ref, kft_ref):
    h = _ada(x_ref[...], g_ref[...], sh_ref[...], sc_ref[...]).astype(BF16)
    z = jnp.dot(h, w1_ref[...], preferred_element_type=F32)
    qt_ref[0] = (z[:, 0:1024] * (MOBA_HEAD_DIM ** -0.5) + qconst_ref[...]).T.astype(BF16)
    mk = z[:, 1024:1152]
    mk_ref[...] = mk
    mv_ref[...] = z[:, 1152:1280]
    ka_ref[...] = (z[:, 1280:1536] + ktab_ref[:, 0:256]).astype(BF16)
    vt_ref[0] = (z[:, 1536:1792] + ktab_ref[:, 256:512]).T.astype(BF16)
    kmean_ref[0] = jnp.mean(mk, axis=0, keepdims=True)
    cos = cos_ref[...]
    sin = sin_ref[...]
    ckv = _rms(z[:, 2048:2176], kvg_ref[...])
    ckv_ref[...] = ckv
    kpe = z[:, 2176:2304] * cos + z[:, 2304:2432] * sin
    kpe_ref[...] = kpe[:, 0:MLA_ROPE_DIM]
    kpe1 = kpe + kones_ref[...]
    kfull_ref[:, 0:128] = ckv.astype(BF16)
    kfull_ref[:, 128:256] = kpe1.astype(BF16)
    kft_ref[0, 0:128, :] = ckv.T.astype(BF16)
    kft_ref[0, 128:256, :] = kpe1.T.astype(BF16)
    cqn = _rms(z[:, 1792:2048], qg_ref[...]).astype(BF16)
    q2 = jnp.dot(cqn, wuq_ref[...], preferred_element_type=F32)
    scale = (MLA_NOPE_DIM + MLA_ROPE_DIM) ** -0.5
    for hd in range(MLA_HEADS):
        nope = q2[:, hd * 64:(hd + 1) * 64].astype(BF16)
        qlat = jnp.dot(nope, wukt_ref[hd], preferred_element_type=F32)
        qft_ref[0, hd * 256:hd * 256 + 128, :] = (qlat * scale).T.astype(BF16)
        pe = q2[:, 512 + hd * 128:512 + (hd + 1) * 128] * cos + q2[:, 1536 + hd * 128:1536 + (hd + 1) * 128] * sin
        qft_ref[0, hd * 256 + 128:(hd + 1) * 256, :] = (pe * scale).T.astype(BF16)


def _inproj_ab(x, shift, scale, g, w1, qg, kvg, wuq, wukt, cos_t, sin_t, qconst, ktab, kones,
               per_row, tm, tiles_per_batch):
    n = x.shape[0]
    nt = n // tm
    row = lambda w: pl.BlockSpec((tm, w), lambda t: (t, 0))
    pos = lambda w: _pos_spec(per_row, tm, w, tiles_per_batch)
    tposed = lambda feats: (jax.ShapeDtypeStruct((nt, feats, tm), BF16),
                            pl.BlockSpec((1, feats, tm), lambda t: (t, 0, 0)))
    outs = [
        tposed(1024),
        (jax.ShapeDtypeStruct((n, 128), F32), row(128)),
        (jax.ShapeDtypeStruct((n, 128), F32), row(128)),
        (jax.ShapeDtypeStruct((n, 256), BF16), row(256)),
        tposed(256),
        (jax.ShapeDtypeStruct((nt, 1, 128), F32), pl.BlockSpec((1, 1, 128), lambda t: (t, 0, 0))),
        tposed(MLA_HEADS * 256),
        (jax.ShapeDtypeStruct((n, 128), F32), row(128)),
        (jax.ShapeDtypeStruct((n, MLA_ROPE_DIM), F32), row(MLA_ROPE_DIM)),
        (jax.ShapeDtypeStruct((n, 256), BF16), row(256)),
        tposed(256),
    ]
    return pl.pallas_call(
        _inproj_ab_kernel,
        grid=(nt,),
        in_specs=[
            row(D_MODEL), _mod_spec(per_row, tm, tiles_per_batch), _mod_spec(per_row, tm, tiles_per_batch),
            _full_spec((1, D_MODEL)), _full_spec((D_MODEL, AB_IN_EXT)), _full_spec((1, MLA_Q_RANK)),
            _full_spec((1, MLA_KV_RANK)), _full_spec((MLA_Q_RANK, UQ_EXT)), _full_spec((MLA_HEADS, 64, 128)),
            pos(LANE), pos(LANE), _full_spec((1, 1024)), pos(512), _full_spec((1, LANE)),
        ],
        out_specs=[o[1] for o in outs],
        out_shape=[o[0] for o in outs],
        compiler_params=_cparams(("arbitrary",)),
        name="inproj_ab",
    )(x, shift, scale, g, w1, qg, kvg, wuq, wukt, cos_t, sin_t, qconst, ktab, kones)


def _inproj_c_kernel(x_ref, sh_ref, sc_ref, g_ref, w_ref, qconst_ref, ktab_ref, qt_ref, k_ref, v_ref, ka_ref, vt_ref):
    h = _ada(x_ref[...], g_ref[...], sh_ref[...], sc_ref[...]).astype(BF16)
    z = jnp.dot(h, w_ref[...], preferred_element_type=F32)
    qt_ref[0] = (z[:, 0:2048] * (DIFF_HEAD_DIM ** -0.5) + qconst_ref[...]).T.astype(BF16)
    k_ref[...] = z[:, 2048:2304]
    v_ref[...] = z[:, 2304:2560]
    ka_ref[...] = (z[:, 2560:3072] + ktab_ref[:, 0:512]).astype(BF16)
    vt_ref[0] = (z[:, 3072:3584] + ktab_ref[:, 512:1024]).T.astype(BF16)


def _inproj_c(x, shift, scale, g, w, qconst, ktab, per_row, tm, tiles_per_batch):
    n = x.shape[0]
    nt = n // tm
    row = lambda wd: pl.BlockSpec((tm, wd), lambda t: (t, 0))
    tspec = lambda feats: pl.BlockSpec((1, feats, tm), lambda t: (t, 0, 0))
    return pl.pallas_call(
        _inproj_c_kernel,
        grid=(nt,),
        in_specs=[row(D_MODEL), _mod_spec(per_row, tm, tiles_per_batch), _mod_spec(per_row, tm, tiles_per_batch),
                  _full_spec((1, D_MODEL)), _full_spec((D_MODEL, C_IN_EXT)), _full_spec((1, 2048)),
                  _pos_spec(per_row, tm, 1024, tiles_per_batch)],
        out_specs=[tspec(2048), row(256), row(256), row(512), tspec(512)],
        out_shape=[jax.ShapeDtypeStruct((nt, 2048, tm), BF16), jax.ShapeDtypeStruct((n, 256), F32),
                   jax.ShapeDtypeStruct((n, 256), F32), jax.ShapeDtypeStruct((n, 512), BF16),
                   jax.ShapeDtypeStruct((nt, 512, tm), BF16)],
        compiler_params=_cparams(("arbitrary",)),
        name="inproj_c",
    )(x, shift, scale, g, w, qconst, ktab)


def _outproj_ab_kernel(x_ref, gate_ref, oa_ref, olat_ref, wuv_ref, wout_ref, o_ref):
    ob = jnp.dot(olat_ref[...], wuv_ref[...], preferred_element_type=F32).astype(BF16)
    y = jnp.dot(oa_ref[...], wout_ref[0:512, :], preferred_element_type=F32)
    y = y + jnp.dot(ob, wout_ref[512:1024, :], preferred_element_type=F32)
    o_ref[...] = x_ref[...] + gate_ref[...] * y


def _outproj_ab(x, gate, oa, olat, wuv_bd, wout, per_row, tm, tiles_per_batch):
    n = x.shape[0]
    row = lambda wd: pl.BlockSpec((tm, wd), lambda t: (t, 0))
    return pl.pallas_call(
        _outproj_ab_kernel,
        grid=(n // tm,),
        in_specs=[row(D_MODEL), _mod_spec(per_row, tm, tiles_per_batch), row(512), row(1024),
                  _full_spec((1024, 512)), _full_spec((1024, D_MODEL))],
        out_specs=row(D_MODEL),
        out_shape=jax.ShapeDtypeStruct((n, D_MODEL), F32),
        compiler_params=_cparams(("arbitrary",)),
        name="outproj_ab",
    )(x, gate, oa, olat, wuv_bd, wout)


def _outproj_c_kernel(x_ref, gate_ref, o_in_ref, wout_ref, o_ref):
    y = jnp.dot(o_in_ref[...], wout_ref[...], preferred_element_type=F32)
    o_ref[...] = x_ref[...] + gate_ref[...] * y


def _outproj_c(x, gate, o_in, wout, per_row, tm, tiles_per_batch):
    n = x.shape[0]
    row = lambda wd: pl.BlockSpec((tm, wd), lambda t: (t, 0))
    return pl.pallas_call(
        _outproj_c_kernel,
        grid=(n // tm,),
        in_specs=[row(D_MODEL), _mod_spec(per_row, tm, tiles_per_batch), row(1024), _full_spec((1024, D_MODEL))],
        out_specs=row(D_MODEL),
        out_shape=jax.ShapeDtypeStruct((n, D_MODEL), F32),
        compiler_params=_cparams(("arbitrary",)),
        name="outproj_c",
    )(x, gate, o_in, wout)


def _final_norm_kernel(x_ref, g_ref, o_ref):
    o_ref[...] = _rms(x_ref[...], g_ref[...])


def _final_norm(x, g, tm):
    n = x.shape[0]
    return pl.pallas_call(
        _final_norm_kernel,
        grid=(n // tm,),
        in_specs=[pl.BlockSpec((tm, D_MODEL), lambda t: (t, 0)), _full_spec((1, D_MODEL))],
        out_specs=pl.BlockSpec((tm, D_MODEL), lambda t: (t, 0)),
        out_shape=jax.ShapeDtypeStruct((n, D_MODEL), F32),
        compiler_params=_cparams(("arbitrary",)),
        name="final_norm",
    )(x, g)


def _router_kernel(x_ref, sh_ref, sc_ref, g_ref, wr_ref, br_ref, hp_ref, info_ref):
    h = _ada(x_ref[...], g_ref[...], sh_ref[...], sc_ref[...])
    bits = lax.bitcast_convert_type(h.astype(BF16).astype(F32), U32)
    hp_ref[...] = bits[:, 0:512] | (bits[:, 512:1024] >> 16)
    logit = jnp.dot(h, wr_ref[...], preferred_element_type=F32, precision=HIGHEST) + br_ref[...]
    lane = lax.broadcasted_iota(I32, logit.shape, 1)
    is_grp = (lane >= N_EXPERTS) & (lane < N_EXPERTS + N_GROUPS)
    lg = jnp.where(is_grp, logit, -jnp.inf)
    mg = jnp.max(lg, axis=-1, keepdims=True)
    g_w = 1.0 / jnp.sum(jnp.exp(lg - mg), axis=-1, keepdims=True)
    g_idx = jnp.min(jnp.where(lg == mg, lane, 4 * LANE), axis=-1, keepdims=True) - N_EXPERTS
    in_grp = (lane < N_EXPERTS) & ((lane // EXPERTS_PER_GROUP) == g_idx)
    le = jnp.where(in_grp, logit, -jnp.inf)
    m1 = jnp.max(le, axis=-1, keepdims=True)
    i1 = jnp.min(jnp.where(le == m1, lane, 4 * LANE), axis=-1, keepdims=True)
    le2 = jnp.where(lane == i1, -jnp.inf, le)
    m2 = jnp.max(le2, axis=-1, keepdims=True)
    i2 = jnp.min(jnp.where(le2 == m2, lane, 4 * LANE), axis=-1, keepdims=True)
    r = jnp.exp(m2 - m1)
    w1 = g_w / (1.0 + r)
    w2 = g_w * r / (1.0 + r)
    info = jnp.where(lane == 0, i1.astype(F32), 0.0) + jnp.where(lane == 1, i2.astype(F32), 0.0)
    info_ref[...] = info + jnp.where(lane == 2, w1, 0.0) + jnp.where(lane == 3, w2, 0.0)


def _router(x, shift, scale, g, wr, br, per_row, tm, tiles_per_batch):
    n = x.shape[0]
    row = lambda wd: pl.BlockSpec((tm, wd), lambda t: (t, 0))
    return pl.pallas_call(
        _router_kernel,
        grid=(n // tm,),
        in_specs=[row(D_MODEL), _mod_spec(per_row, tm, tiles_per_batch), _mod_spec(per_row, tm, tiles_per_batch),
                  _full_spec((1, D_MODEL)), _full_spec((D_MODEL, LANE)), _full_spec((1, LANE))],
        out_specs=[row(512), row(LANE)],
        out_shape=[jax.ShapeDtypeStruct((n, 512), U32), jax.ShapeDtypeStruct((n, LANE), F32)],
        compiler_params=_cparams(("arbitrary",)),
        name="router",
    )(x, shift, scale, g, wr, br)


def _routing_offsets(expert_ids, n_tiles_max):
    onehot = (expert_ids[:, None] == jnp.arange(N_EXPERTS, dtype=I32)[None, :]).astype(I32)
    csum = jnp.cumsum(onehot, axis=0)
    counts = csum[-1]
    rank = jnp.sum(onehot * csum, axis=1) - 1
    padded = ((counts + EXPERT_TILE - 1) // EXPERT_TILE) * EXPERT_TILE
    ends = jnp.cumsum(padded)
    starts = ends - padded
    dest = jnp.sum(onehot * starts[None, :], axis=1) + rank
    tile_rows = jnp.arange(n_tiles_max, dtype=I32) * EXPERT_TILE
    tile_expert = jnp.minimum(jnp.sum((tile_rows[:, None] >= ends[None, :]).astype(I32), axis=1), N_EXPERTS - 1)
    n_tiles = (ends[-1] // EXPERT_TILE).reshape(1)
    return dest.astype(I32), tile_expert.astype(I32), n_tiles.astype(I32)


def _dispatch_kernel(dest_ref, hp_ref, xs_in_ref, xs_ref, sem, *, tm):
    del xs_in_ref

    def row_copy(r, k):
        d = dest_ref[0, 0, TOP_EXPERTS * r + k]
        return pltpu.make_async_copy(hp_ref.at[pl.ds(r, 1)], xs_ref.at[pl.ds(d, 1)], sem.at[0])

    def start(r, c):
        for k in range(TOP_EXPERTS):
            row_copy(r, k).start()
        return c

    lax.fori_loop(0, tm, start, 0, unroll=8)
    for k in range(TOP_EXPERTS):
        pltpu.make_async_copy(hp_ref, xs_ref.at[pl.ds(0, tm)], sem.at[0]).wait()


def _dispatch(dest_tiles, hp, xs, tm):
    n = hp.shape[0]
    return pl.pallas_call(
        functools.partial(_dispatch_kernel, tm=tm),
        grid=(n // tm,),
        in_specs=[pl.BlockSpec((1, 1, TOP_EXPERTS * tm), lambda t: (t, 0, 0), memory_space=pltpu.SMEM),
                  pl.BlockSpec((tm, 512), lambda t: (t, 0)),
                  pl.BlockSpec(memory_space=pl.ANY)],
        out_specs=pl.BlockSpec(memory_space=pl.ANY),
        out_shape=jax.ShapeDtypeStruct(xs.shape, U32),
        scratch_shapes=[pltpu.SemaphoreType.DMA((1,))],
        input_output_aliases={2: 0},
        compiler_params=_cparams(("arbitrary",)),
        name="moe_dispatch",
    )(dest_tiles, hp, xs)


def _experts_kernel(te_ref, nt_ref, xs_ref, w1_ref, w3_ref, w2_ref, y_ref):
    del te_ref

    @pl.when(pl.program_id(0) < nt_ref[0])
    def _():
        packed = xs_ref[...]
        hi = lax.bitcast_convert_type(packed & jnp.uint32(0xFFFF0000), F32).astype(BF16)
        lo = lax.bitcast_convert_type(packed << 16, F32).astype(BF16)
        x = jnp.concatenate([hi, lo], axis=1)
        a = jnp.dot(x, w1_ref[0].astype(BF16), preferred_element_type=F32)
        b = jnp.dot(x, w3_ref[0].astype(BF16), preferred_element_type=F32)
        hid = (a / (1.0 + jnp.exp(-a))) * b
        y_ref[...] = jnp.dot(hid.astype(BF16), w2_ref[0].astype(BF16), preferred_element_type=F32)

    @pl.when(pl.program_id(0) >= nt_ref[0])
    def _():
        y_ref[...] = jnp.zeros_like(y_ref)


def _experts(tile_expert, n_tiles, xs, w1, w3, w2):
    n_tiles_max = xs.shape[0] // EXPERT_TILE
    live = lambda t, te, nt: jnp.minimum(t, nt[0] - 1)
    wspec = lambda a, b: pl.BlockSpec((1, a, b), lambda t, te, nt: (te[live(t, te, nt)], 0, 0))
    return pl.pallas_call(
        _experts_kernel,
        grid_spec=pltpu.PrefetchScalarGridSpec(
            num_scalar_prefetch=2,
            grid=(n_tiles_max,),
            in_specs=[pl.BlockSpec((EXPERT_TILE, 512), lambda t, te, nt: (live(t, te, nt), 0)),
                      wspec(D_MODEL, EXPERT_FF), wspec(D_MODEL, EXPERT_FF), wspec(EXPERT_FF, D_MODEL)],
            out_specs=pl.BlockSpec((EXPERT_TILE, D_MODEL), lambda t, te, nt: (t, 0))),
        out_shape=jax.ShapeDtypeStruct((xs.shape[0], D_MODEL), F32),
        compiler_params=_cparams(("arbitrary",)),
        name="moe_experts",
    )(tile_expert, n_tiles, xs, w1, w3, w2)


def _combine_kernel(dest_ref, dest_next_ref, info_ref, x_ref, gmod_ref, y_hbm, o_ref, ybuf, sem, *, tm):
    t = pl.program_id(0)
    nt = pl.num_programs(0)
    slot = t % 2

    def row_copy(dref, slot_, r, k):
        d = dref[0, 0, TOP_EXPERTS * r + k]
        return pltpu.make_async_copy(y_hbm.at[pl.ds(d, 1)], ybuf.at[slot_, pl.ds(k * tm + r, 1)], sem.at[slot_])

    def start_all(dref, slot_):
        def body(r, c):
            for k in range(TOP_EXPERTS):
                row_copy(dref, slot_, r, k).start()
            return c
        lax.fori_loop(0, tm, body, 0, unroll=8)

    @pl.when(t == 0)
    def _():
        start_all(dest_ref, 0)

    @pl.when(t + 1 < nt)
    def _():
        start_all(dest_next_ref, 1 - slot)

    pltpu.make_async_copy(y_hbm.at[pl.ds(0, TOP_EXPERTS * tm)], ybuf.at[slot], sem.at[slot]).wait()
    info = info_ref[...]
    y = info[:, 2:3] * ybuf[slot, 0:tm] + info[:, 3:4] * ybuf[slot, tm:2 * tm]
    o_ref[...] = x_ref[...] + gmod_ref[...] * y


def _combine(dest_tiles, info, x, gmod, y_sorted, per_row, tm, tiles_per_batch):
    n = x.shape[0]
    nt = n // tm
    row = lambda wd: pl.BlockSpec((tm, wd), lambda t: (t, 0))
    dspec = lambda f: pl.BlockSpec((1, 1, TOP_EXPERTS * tm), lambda t: (f(t), 0, 0), memory_space=pltpu.SMEM)
    return pl.pallas_call(
        functools.partial(_combine_kernel, tm=tm),
        grid=(nt,),
        in_specs=[dspec(lambda t: t), dspec(lambda t: jnp.minimum(t + 1, nt - 1)), row(LANE), row(D_MODEL),
                  _mod_spec(per_row, tm, tiles_per_batch), pl.BlockSpec(memory_space=pl.ANY)],
        out_specs=row(D_MODEL),
        out_shape=jax.ShapeDtypeStruct((n, D_MODEL), F32),
        scratch_shapes=[pltpu.VMEM((2, TOP_EXPERTS * tm, D_MODEL), F32), pltpu.SemaphoreType.DMA((2,))],
        compiler_params=_cparams(("arbitrary",)),
        name="moe_combine",
    )(dest_tiles, dest_tiles, info, x, gmod, y_sorted)


def _qk(q, k):
    return lax.dot_general(q, k, (((1,), (1,)), ((), ())), preferred_element_type=F32)


def _causal_bias_t(t, copies):
    k = lax.broadcasted_iota(I32, (t, t), 0)
    q = lax.broadcasted_iota(I32, (t, t), 1)
    return jnp.concatenate([jnp.where(k <= q, 0.0, NEG)] * copies, axis=1)


def _online_softmax_step_t(m_ref, acc_ref, idx, s_t, v_t):
    m_old = m_ref[idx]
    m_new = jnp.maximum(m_old, jnp.max(s_t, axis=0, keepdims=True))
    p = jnp.exp(s_t - m_new)
    acc_ref[idx] = jnp.exp(m_old - m_new) * acc_ref[idx] + jnp.dot(v_t, p.astype(BF16), preferred_element_type=F32)
    m_ref[idx] = m_new


def _moba_prompt_kernel(qt_ref, ka_ref, vt_ref, km_ref, o_ref, q_scr, m_ref, acc_ref, *, nb):
    i = pl.program_id(1)
    blk_rows = MOBA_BLOCK
    rep = MOBA_HEADS // MOBA_KV_HEADS
    cols = rep * blk_rows
    blk = lax.broadcasted_iota(I32, (SEL_LANES, cols), 0)
    for g in range(MOBA_KV_HEADS):
        q_t = jnp.concatenate([qt_ref[0, (g * rep + j) * LANE:(g * rep + j + 1) * LANE, :] for j in range(rep)],
                              axis=1)
        gate = jnp.dot(km_ref[0, g], q_t.astype(F32), preferred_element_type=F32, precision=HIGHEST)
        valid = blk < i
        gate = jnp.where(valid, gate, -jnp.inf)
        rank = jnp.zeros((SEL_LANES, cols), I32)
        for mb in range(nb):
            gm = gate[mb:mb + 1, :]
            beats = (gm > gate) | ((gm == gate) & (blk > mb))
            rank = rank + beats.astype(I32)
        keep = (valid & (rank < MOBA_TOPK)) | (blk == i)
        bias = jnp.where((blk < nb) & jnp.logical_not(keep), NEG, 0.0)
        q_scr[g] = jnp.concatenate([q_t[0:SEL_LANE0], bias.astype(BF16), q_t[SEL_LANE0 + SEL_LANES:LANE]], axis=0)
        m_ref[g] = jnp.full((1, cols), NEG, F32)
        acc_ref[g] = jnp.zeros((MOBA_V_ROWS, cols), F32)

    def step(j, g, bias_t):
        off = pl.multiple_of(j * blk_rows, blk_rows)
        s_t = jnp.dot(ka_ref[pl.ds(off, blk_rows), g * LANE:(g + 1) * LANE], q_scr[g], preferred_element_type=F32)
        if bias_t is not None:
            s_t = s_t + bias_t
        _online_softmax_step_t(m_ref, acc_ref, g, s_t, vt_ref[j, g * LANE:g * LANE + MOBA_V_ROWS, :])

    def past(j, c):
        for g in range(MOBA_KV_HEADS):
            step(j, g, None)
        return c

    lax.fori_loop(0, i, past, 0)
    causal_t = _causal_bias_t(blk_rows, rep)
    for g in range(MOBA_KV_HEADS):
        step(i, g, causal_t)
        acc = acc_ref[g]
        o_t = acc[0:MOBA_HEAD_DIM] / acc[ONES_LANE:ONES_LANE + 1]
        o_t = jnp.concatenate([o_t, jnp.zeros_like(o_t)], axis=0)
        for j in range(rep):
            hd = g * rep + j
            o_ref[:, hd * 64:(hd + 1) * 64] = o_t[:, j * blk_rows:(j + 1) * blk_rows].T[:, 0:64].astype(BF16)


def _moba_prompt(qt, ka, vt, km, batch, seq):
    nb = seq // MOBA_BLOCK
    cols = (MOBA_HEADS // MOBA_KV_HEADS) * MOBA_BLOCK
    return pl.pallas_call(
        functools.partial(_moba_prompt_kernel, nb=nb),
        grid=(batch, nb),
        in_specs=[pl.BlockSpec((1, 1024, MOBA_BLOCK), lambda b, i: (b * nb + i, 0, 0)),
                  pl.BlockSpec((seq, 256), lambda b, i: (b, 0)),
                  pl.BlockSpec((nb, 256, MOBA_BLOCK), lambda b, i: (b, 0, 0)),
                  pl.BlockSpec((1, MOBA_KV_HEADS, SEL_LANES, LANE), lambda b, i: (b, 0, 0, 0))],
        out_specs=pl.BlockSpec((MOBA_BLOCK, 512), lambda b, i: (b * nb + i, 0)),
        out_shape=jax.ShapeDtypeStruct((batch * seq, 512), BF16),
        scratch_shapes=[pltpu.VMEM((MOBA_KV_HEADS, LANE, cols), BF16), pltpu.VMEM((MOBA_KV_HEADS, 1, cols), F32),
                        pltpu.VMEM((MOBA_KV_HEADS, MOBA_V_ROWS, cols), F32)],
        compiler_params=_cparams(("arbitrary", "arbitrary")),
        name="moba_prompt",
    )(qt, ka, vt, km)


def _mla_prompt_kernel(qt_ref, k_ref, kt_ref, o_ref, q_scr, m_ref, acc_ref):
    i = pl.program_id(1)
    tq = TOKEN_TILE
    cols = MLA_HEADS * tq
    q_scr[...] = jnp.concatenate([qt_ref[0, hd * 256:(hd + 1) * 256, :] for hd in range(MLA_HEADS)], axis=1)
    m_ref[0] = jnp.full((1, cols), NEG, F32)
    acc_ref[0] = jnp.zeros((MLA_V_ROWS, cols), F32)

    def step(j, bias_t):
        s_t = jnp.dot(k_ref[pl.ds(pl.multiple_of(j * tq, tq), tq), :], q_scr[...], preferred_element_type=F32)
        if bias_t is not None:
            s_t = s_t + bias_t
        _online_softmax_step_t(m_ref, acc_ref, 0, s_t, kt_ref[j, 0:MLA_V_ROWS, :])

    def past(j, c):
        step(j, None)
        return c

    lax.fori_loop(0, i, past, 0)
    step(i, _causal_bias_t(tq, MLA_HEADS))
    acc = acc_ref[0]
    o_t = acc[0:MLA_KV_RANK] / acc[MLA_ONES_LANE:MLA_ONES_LANE + 1]
    for hd in range(MLA_HEADS):
        o_ref[:, hd * 128:(hd + 1) * 128] = o_t[:, hd * tq:(hd + 1) * tq].T.astype(BF16)


def _mla_prompt(qft, kfull, kft, batch, seq):
    nq = seq // TOKEN_TILE
    cols = MLA_HEADS * TOKEN_TILE
    return pl.pallas_call(
        _mla_prompt_kernel,
        grid=(batch, nq),
        in_specs=[pl.BlockSpec((1, MLA_HEADS * 256, TOKEN_TILE), lambda b, i: (b * nq + i, 0, 0)),
                  pl.BlockSpec((seq, 256), lambda b, i: (b, 0)),
                  pl.BlockSpec((nq, 256, TOKEN_TILE), lambda b, i: (b, 0, 0))],
        out_specs=pl.BlockSpec((TOKEN_TILE, 1024), lambda b, i: (b * nq + i, 0)),
        out_shape=jax.ShapeDtypeStruct((batch * seq, 1024), BF16),
        scratch_shapes=[pltpu.VMEM((256, cols), BF16), pltpu.VMEM((1, 1, cols), F32),
                        pltpu.VMEM((1, MLA_V_ROWS, cols), F32)],
        compiler_params=_cparams(("arbitrary", "arbitrary")),
        name="mla_prompt",
    )(qft, kfull, kft)


def _lambda_full(lam_ref, lam_init):
    lam = lam_ref[...]
    a = jnp.sum(lam[0:1] * lam[1:2], axis=-1, keepdims=True)
    b = jnp.sum(lam[2:3] * lam[3:4], axis=-1, keepdims=True)
    return jnp.exp(a) - jnp.exp(b) + lam_init


def _diff_prompt_kernel(qt_ref, ka_ref, vt_ref, lam_ref, subg_ref, o_ref, q_scr, m_ref, acc_ref, *, lam_init):
    i = pl.program_id(1)
    tq = TOKEN_TILE
    rep = DIFF_HEADS // DIFF_KV_HEADS
    cols = rep * tq
    chains = [(g, c) for g in range(DIFF_KV_HEADS) for c in range(2)]
    for n, (g, c) in enumerate(chains):
        q_scr[n] = jnp.concatenate(
            [qt_ref[0, ((g * rep + j) * 2 + c) * LANE:((g * rep + j) * 2 + c + 1) * LANE, :] for j in range(rep)],
            axis=1)
        m_ref[n] = jnp.full((1, cols), NEG, F32)
        acc_ref[n] = jnp.zeros((DIFF_V_ROWS, cols), F32)

    def step(j, n, bias_t):
        g, c = chains[n]
        off = pl.multiple_of(j * tq, tq)
        s_t = jnp.dot(ka_ref[pl.ds(off, tq), (g * 2 + c) * LANE:(g * 2 + c + 1) * LANE], q_scr[n],
                      preferred_element_type=F32)
        if bias_t is not None:
            s_t = s_t + bias_t
        _online_softmax_step_t(m_ref, acc_ref, n, s_t, vt_ref[j, g * 256:g * 256 + DIFF_V_ROWS, :])

    def past(j, cc):
        for n in range(len(chains)):
            step(j, n, None)
        return cc

    lax.fori_loop(0, i, past, 0)
    causal_t = _causal_bias_t(tq, rep)
    lam_full = _lambda_full(lam_ref, lam_init)
    outs = []
    for n in range(len(chains)):
        step(i, n, causal_t)
        acc = acc_ref[n]
        outs.append(acc[0:128] / acc[128:129])
    for g in range(DIFF_KV_HEADS):
        o_t = outs[2 * g] - lam_full * outs[2 * g + 1]
        o_t = o_t * lax.rsqrt(jnp.mean(o_t * o_t, axis=0, keepdims=True) + EPS)
        for j in range(rep):
            hd = g * rep + j
            o = o_t[:, j * tq:(j + 1) * tq].T * subg_ref[...] * (1.0 - lam_init)
            o_ref[:, hd * 128:(hd + 1) * 128] = o.astype(BF16)


def _diff_prompt(qt, ka, vt, lam, subg, lam_init, batch, seq):
    nq = seq // TOKEN_TILE
    cols = (DIFF_HEADS // DIFF_KV_HEADS) * TOKEN_TILE
    n_chain = DIFF_KV_HEADS * 2
    return pl.pallas_call(
        functools.partial(_diff_prompt_kernel, lam_init=lam_init),
        grid=(batch, nq),
        in_specs=[pl.BlockSpec((1, 2048, TOKEN_TILE), lambda b, i: (b * nq + i, 0, 0)),
                  pl.BlockSpec((seq, 512), lambda b, i: (b, 0)),
                  pl.BlockSpec((nq, 512, TOKEN_TILE), lambda b, i: (b, 0, 0)),
                  pl.BlockSpec((4, 64), lambda b, i: (0, 0)),
                  pl.BlockSpec((1, 128), lambda b, i: (0, 0))],
        out_specs=pl.BlockSpec((TOKEN_TILE, 1024), lambda b, i: (b * nq + i, 0)),
        out_shape=jax.ShapeDtypeStruct((batch * seq, 1024), BF16),
        scratch_shapes=[pltpu.VMEM((n_chain, LANE, cols), BF16), pltpu.VMEM((n_chain, 1, cols), F32),
                        pltpu.VMEM((n_chain, DIFF_V_ROWS, cols), F32)],
        compiler_params=_cparams(("arbitrary", "arbitrary")),
        name="diff_prompt",
    )(qt, ka, vt, lam, subg)


def _page_copy(src_hbm, page, dst, sem):
    return pltpu.make_async_copy(src_hbm.at[page], dst, sem)


def _paged_pipeline(pt_ref, n_pages, layer_off, streams, sem):
    s = pl.program_id(0)
    ns = pl.num_programs(0)
    slot = s % 2

    def start(seq, slot_):
        for p in range(n_pages):
            page = pt_ref[seq * n_pages + p] + layer_off
            for hbm, buf, dst_fn in streams:
                _page_copy(hbm, page, dst_fn(buf, slot_, p), sem.at[slot_]).start()

    @pl.when(s == 0)
    def _():
        start(0, 0)

    @pl.when(s + 1 < ns)
    def _():
        start(s + 1, 1 - slot)

    for p in range(n_pages):
        for hbm, buf, dst_fn in streams:
            _page_copy(hbm, 0, dst_fn(buf, slot, p), sem.at[slot]).wait()
    return slot


def _lane_window(buf, slot, p):
    return buf.at[slot, :, pl.ds(p * LANE, LANE)]


def _row_window(rows):
    return lambda buf, slot, p: buf.at[slot, pl.ds(p * rows, rows), :]


def _moba_decode_kernel(pt_ref, q_ref, knew_ref, vnew_ref, k_hbm, v_hbm, o_ref, kbuf, vbuf, sem, *,
                        n_pages, layer_off, past_len):
    slot = _paged_pipeline(pt_ref, n_pages, layer_off, [(k_hbm, kbuf, _lane_window), (v_hbm, vbuf, _lane_window)],
                           sem)
    nblk = past_len // MOBA_BLOCK
    q = q_ref[0]
    kt = kbuf[slot].astype(BF16)
    s_raw = jnp.dot(q, kt, preferred_element_type=F32)
    lane = lax.broadcasted_iota(I32, (MOBA_HEADS, LANE), 1)
    gate = jnp.full((MOBA_HEADS, LANE), -jnp.inf, F32)
    for b in range(nblk):
        gs = jnp.sum(s_raw[:, b * MOBA_BLOCK:(b + 1) * MOBA_BLOCK], axis=-1, keepdims=True)
        gate = jnp.where(lane == b, gs, gate)
    rank = jnp.zeros((MOBA_HEADS, LANE), I32)
    for mb in range(nblk):
        gm = gate[:, mb:mb + 1]
        beats = (gm > gate) | ((gm == gate) & (lane > mb))
        rank = rank + beats.astype(I32)
    sel = (lane < nblk) & (rank < MOBA_TOPK)
    selmask = jnp.concatenate(
        [jnp.broadcast_to(jnp.sum(jnp.where((lane == b) & sel, 1.0, 0.0), axis=-1, keepdims=True) > 0.5,
                          (MOBA_HEADS, MOBA_BLOCK)) for b in range(nblk)], axis=1)
    slope = _slope_column(_alibi_slopes(MOBA_HEADS))
    pos = lax.broadcasted_iota(I32, (MOBA_HEADS, past_len), 1)
    dist = (past_len - pos).astype(F32)
    s = jnp.where(selmask, s_raw - slope * dist, NEG)
    s_own = jnp.sum(q.astype(F32) * knew_ref[0], axis=-1, keepdims=True)
    m = jnp.maximum(jnp.max(s, axis=-1, keepdims=True), s_own)
    p = jnp.exp(s - m)
    p_own = jnp.exp(s_own - m)
    l = jnp.sum(p, axis=-1, keepdims=True) + p_own
    vt = vbuf[slot].astype(BF16)
    acc = _qk(p.astype(BF16), vt) + p_own * vnew_ref[0]
    o_ref[0] = acc / l


def _moba_decode(pt_flat, q_bd, knew, vnew, k_pages, v_pages, layer, n_pool, n_seq, n_pages):
    past_len = n_pages * LANE
    kern = functools.partial(_moba_decode_kernel, n_pages=n_pages, layer_off=layer * n_pool, past_len=past_len)
    return pl.pallas_call(
        kern,
        grid_spec=pltpu.PrefetchScalarGridSpec(
            num_scalar_prefetch=1,
            grid=(n_seq,),
            in_specs=[pl.BlockSpec((1, 8, 128), lambda s, pt: (s, 0, 0)),
                      pl.BlockSpec((1, 1, 128), lambda s, pt: (s, 0, 0)),
                      pl.BlockSpec((1, 1, 128), lambda s, pt: (s, 0, 0)),
                      pl.BlockSpec(memory_space=pl.ANY), pl.BlockSpec(memory_space=pl.ANY)],
            out_specs=pl.BlockSpec((1, 8, 128), lambda s, pt: (s, 0, 0)),
            scratch_shapes=[pltpu.VMEM((2, 128, past_len), F32), pltpu.VMEM((2, 128, past_len), F32),
                            pltpu.SemaphoreType.DMA((2,))]),
        out_shape=jax.ShapeDtypeStruct((n_seq, 8, 128), F32),
        compiler_params=_cparams(("arbitrary",)),
        name="moba_decode",
    )(pt_flat, q_bd, knew, vnew, k_pages, v_pages)


def _mla_decode_kernel(pt_ref, qlat_ref, qpe_ref, cnew_ref, pnew_ref, c_hbm, p_hbm, o_ref, cbuf, pbuf, sem, *,
                       n_pages, layer_off):
    slot = _paged_pipeline(pt_ref, n_pages, layer_off,
                           [(c_hbm, cbuf, _row_window(LANE)), (p_hbm, pbuf, _lane_window)], sem)
    qlat = qlat_ref[0]
    qpe = qpe_ref[0]
    ckv = cbuf[slot].astype(BF16)
    kpet = pbuf[slot].astype(BF16)
    s = _qk(qlat, ckv) + jnp.dot(qpe, kpet, preferred_element_type=F32)
    s_own = (jnp.sum(qlat.astype(F32) * cnew_ref[0], axis=-1, keepdims=True)
             + jnp.sum(qpe.astype(F32) * pnew_ref[0], axis=-1, keepdims=True))
    m = jnp.maximum(jnp.max(s, axis=-1, keepdims=True), s_own)
    p = jnp.exp(s - m)
    p_own = jnp.exp(s_own - m)
    l = jnp.sum(p, axis=-1, keepdims=True) + p_own
    acc = jnp.dot(p.astype(BF16), ckv, preferred_element_type=F32) + p_own * cnew_ref[0]
    o_ref[0] = (acc / l).astype(BF16)


def _mla_decode(pt_flat, qlat, qpe, cnew, pnew, c_pages, p_pages, layer, n_pool, n_seq, n_pages):
    past_len = n_pages * LANE
    kern = functools.partial(_mla_decode_kernel, n_pages=n_pages, layer_off=layer * n_pool)
    return pl.pallas_call(
        kern,
        grid_spec=pltpu.PrefetchScalarGridSpec(
            num_scalar_prefetch=1,
            grid=(n_seq,),
            in_specs=[pl.BlockSpec((1, 8, 128), lambda s, pt: (s, 0, 0)),
                      pl.BlockSpec((1, 8, 32), lambda s, pt: (s, 0, 0)),
                      pl.BlockSpec((1, 1, 128), lambda s, pt: (s, 0, 0)),
                      pl.BlockSpec((1, 1, 32), lambda s, pt: (s, 0, 0)),
                      pl.BlockSpec(memory_space=pl.ANY), pl.BlockSpec(memory_space=pl.ANY)],
            out_specs=pl.BlockSpec((1, 8, 128), lambda s, pt: (s, 0, 0)),
            scratch_shapes=[pltpu.VMEM((2, past_len, 128), F32), pltpu.VMEM((2, MLA_ROPE_DIM, past_len), F32),
                            pltpu.SemaphoreType.DMA((2,))]),
        out_shape=jax.ShapeDtypeStruct((n_seq, 8, 128), BF16),
        compiler_params=_cparams(("arbitrary",)),
        name="mla_decode",
    )(pt_flat, qlat, qpe, cnew, pnew, c_pages, p_pages)


def _diff_decode_kernel(pt_ref, q_ref, knew_ref, vnew_ref, lam_ref, subg_ref, k_hbm, v_hbm, o_ref,
                        kbuf, vbuf, sem, m_ref, l_ref, acc_ref, *, n_pages, pages_per_step, layer_off, lam_init):
    st = pl.program_id(0)
    halves = n_pages // pages_per_step
    half = st % halves
    slot = _paged_pipeline(pt_ref, pages_per_step, layer_off,
                           [(k_hbm, kbuf, _lane_window), (v_hbm, vbuf, _row_window(2 * LANE))], sem)
    chunk = pages_per_step * LANE
    q = q_ref[0]
    kt = kbuf[slot].astype(BF16)
    head_slopes = _alibi_slopes(DIFF_HEADS)
    slope = _slope_column([head_slopes[(r // 8) * 4 + (r % 4)] for r in range(16)])
    pos = lax.broadcasted_iota(I32, (16, chunk), 1) + half * chunk
    dist = (n_pages * LANE - pos).astype(F32)
    s = jnp.dot(q, kt, preferred_element_type=F32) - slope * dist

    @pl.when(half == 0)
    def _():
        m_ref[...] = jnp.full_like(m_ref, NEG)
        l_ref[...] = jnp.zeros_like(l_ref)
        acc_ref[...] = jnp.zeros_like(acc_ref)

    m_old = m_ref[...]
    m_new = jnp.maximum(m_old, jnp.max(s, axis=-1, keepdims=True))
    alpha = jnp.exp(m_old - m_new)
    p = jnp.exp(s - m_new)
    l_ref[...] = alpha * l_ref[...] + jnp.sum(p, axis=-1, keepdims=True)
    pb = p.astype(BF16)
    pv = []
    for g in range(DIFF_KV_HEADS):
        vg = vbuf[slot, pl.ds(g, chunk, stride=2), :].astype(BF16)
        pv.append(jnp.dot(pb[g * 8:(g + 1) * 8], vg, preferred_element_type=F32))
    acc_ref[...] = alpha * acc_ref[...] + jnp.concatenate(pv, axis=0)
    m_ref[...] = m_new

    @pl.when(half == halves - 1)
    def _():
        s_own = jnp.sum(q.astype(F32) * knew_ref[0], axis=-1, keepdims=True)
        m_o = m_ref[...]
        m_f = jnp.maximum(m_o, s_own)
        a = jnp.exp(m_o - m_f)
        p_own = jnp.exp(s_own - m_f)
        l = a * l_ref[...] + p_own
        vnew = vnew_ref[0]
        vrows = jnp.concatenate([jnp.broadcast_to(vnew[:, g * 128:(g + 1) * 128], (8, 128))
                                 for g in range(DIFF_KV_HEADS)], axis=0)
        o = (a * acc_ref[...] + p_own * vrows) / l
        lam_full = _lambda_full(lam_ref, lam_init)
        res = []
        for g in range(DIFF_KV_HEADS):
            o1 = o[g * 8:g * 8 + 4]
            o2 = o[g * 8 + 4:g * 8 + 8]
            res.append(_rms(o1 - lam_full * o2, subg_ref[...]) * (1.0 - lam_init))
        o_ref[0] = jnp.concatenate(res, axis=0).astype(BF16)


def _diff_decode(pt_flat, q_bd, knew, vnew, lam, subg, k_pages, v_pages, layer, n_pool, n_seq, n_pages, lam_init):
    halves = 2
    pps = n_pages // halves
    chunk = pps * LANE
    kern = functools.partial(_diff_decode_kernel, n_pages=n_pages, pages_per_step=pps, layer_off=layer * n_pool,
                             lam_init=lam_init)
    return pl.pallas_call(
        kern,
        grid_spec=pltpu.PrefetchScalarGridSpec(
            num_scalar_prefetch=1,
            grid=(n_seq * halves,),
            in_specs=[pl.BlockSpec((1, 16, 256), lambda s, pt: (s // halves, 0, 0)),
                      pl.BlockSpec((1, 1, 256), lambda s, pt: (s // halves, 0, 0)),
                      pl.BlockSpec((1, 1, 256), lambda s, pt: (s // halves, 0, 0)),
                      pl.BlockSpec((4, 64), lambda s, pt: (0, 0)),
                      pl.BlockSpec((1, 128), lambda s, pt: (0, 0)),
                      pl.BlockSpec(memory_space=pl.ANY), pl.BlockSpec(memory_space=pl.ANY)],
            out_specs=pl.BlockSpec((1, 8, 128), lambda s, pt: (s // halves, 0, 0)),
            scratch_shapes=[pltpu.VMEM((2, 256, chunk), F32), pltpu.VMEM((2, 2 * chunk, 128), F32),
                            pltpu.SemaphoreType.DMA((2,)),
                            pltpu.VMEM((16, 1), F32), pltpu.VMEM((16, 1), F32), pltpu.VMEM((16, 128), F32)]),
        out_shape=jax.ShapeDtypeStruct((n_seq, 8, 128), BF16),
        compiler_params=_cparams(("arbitrary",)),
        name="diff_decode",
    )(pt_flat, q_bd, knew, vnew, lam, subg, k_pages, v_pages)


def _rope_tables(pos):
    half = MLA_ROPE_DIM // 2
    inv = ROPE_BASE ** (-jnp.arange(half, dtype=F32) / half)
    ang = pos.astype(F32)[:, None] * inv[None, :]
    pad = jnp.zeros((pos.shape[0], LANE - MLA_ROPE_DIM), F32)
    cos, sin = jnp.cos(ang), jnp.sin(ang)
    return jnp.concatenate([cos, cos, pad], axis=1), jnp.concatenate([sin, sin, pad], axis=1)


def _swap_halves(w):
    half = w.shape[-1] // 2
    return jnp.concatenate([-w[..., half:], w[..., :half]], axis=-1)


def _pad_cols(w, width):
    return jnp.pad(w, [(0, 0)] * (w.ndim - 1) + [(0, width - w.shape[-1])])


def _pad_blocks(w, block, width):
    rows = w.shape[0]
    return _pad_cols(w.reshape(rows, -1, block), width).reshape(rows, -1)


def _ab_weights(w_in, w_uq, w_uk, w_uv):
    wq, wk, wv = w_in[:, 0:512], w_in[:, 512:640], w_in[:, 640:768]
    kpe_w = w_in[:, 1152:1184]
    w1 = jnp.concatenate([_pad_blocks(wq, 64, LANE), wk, wv, _pad_blocks(wk, 64, LANE), _pad_blocks(wv, 64, LANE),
                          w_in[:, 768:1152], _pad_cols(kpe_w, LANE), _pad_cols(_swap_halves(kpe_w), LANE)], axis=1)
    uq = w_uq.reshape(MLA_Q_RANK, MLA_HEADS, MLA_NOPE_DIM + MLA_ROPE_DIM)
    nope = uq[:, :, :MLA_NOPE_DIM].reshape(MLA_Q_RANK, MLA_HEADS * MLA_NOPE_DIM)
    pe = uq[:, :, MLA_NOPE_DIM:]
    pe_pad = _pad_cols(pe, LANE).reshape(MLA_Q_RANK, MLA_HEADS * LANE)
    pes_pad = _pad_cols(_swap_halves(pe), LANE).reshape(MLA_Q_RANK, MLA_HEADS * LANE)
    wuq = jnp.concatenate([nope, pe_pad, pes_pad], axis=1)
    wukt = jnp.transpose(w_uk, (1, 2, 0))
    eye = jnp.eye(MLA_HEADS, dtype=F32)
    wuv_bd = (jnp.transpose(w_uv, (1, 0, 2))[:, :, None, :] * eye[:, None, :, None]).reshape(
        MLA_HEADS * MLA_KV_RANK, MLA_HEADS * MLA_V_DIM)
    return w1.astype(BF16), wuq.astype(BF16), wukt.astype(BF16), wuv_bd.astype(BF16)


def _c_weights(w_in):
    wq, wk, wv = w_in[:, 0:1024], w_in[:, 1024:1280], w_in[:, 1280:1536]
    w = jnp.concatenate([_pad_blocks(wq, 64, LANE), wk, wv, _pad_blocks(wk, 64, LANE), _pad_blocks(wv, 128, 256)],
                        axis=1)
    return w.astype(BF16)


def _lane_table(rows, width, entries):
    lane = jnp.arange(width, dtype=I32)[None, :]
    tab = jnp.zeros((rows, width), F32)
    for ln, val in entries.items():
        tab = jnp.where(lane == ln, jnp.broadcast_to(jnp.asarray(val, F32).reshape(-1, 1), (rows, 1)), tab)
    return tab


def _query_consts(slopes, blocks_per_head):
    ent = {}
    for h, s in enumerate(slopes):
        for c in range(blocks_per_head):
            base = (h * blocks_per_head + c) * LANE
            ent[base + POS_HI_LANE] = s * 256.0
            ent[base + POS_LO_LANE] = s
    return _lane_table(1, len(slopes) * blocks_per_head * LANE, ent)


def _key_tables(seq):
    pos = jnp.arange(seq, dtype=I32)
    hi = (pos // 256).astype(F32)
    lo = (pos % 256).astype(F32)
    nb = seq // MOBA_BLOCK
    ent = {}
    for g in range(MOBA_KV_HEADS):
        ent[g * LANE + POS_HI_LANE] = hi
        ent[g * LANE + POS_LO_LANE] = lo
        for n in range(nb):
            ent[g * LANE + SEL_LANE0 + n] = (pos // MOBA_BLOCK == n).astype(F32)
        ent[(MOBA_KV_HEADS + g) * LANE + ONES_LANE] = jnp.ones((seq,), F32)
    ktab_ab = _lane_table(seq, 2 * MOBA_KV_HEADS * LANE, ent)
    ent = {}
    for gc in range(2 * DIFF_KV_HEADS):
        ent[gc * LANE + POS_HI_LANE] = hi
        ent[gc * LANE + POS_LO_LANE] = lo
    for g in range(DIFF_KV_HEADS):
        ent[512 + g * 256 + 128] = jnp.ones((seq,), F32)
    ktab_c = _lane_table(seq, 1024, ent)
    return ktab_ab, ktab_c


def _block_diag_rows(q, blocks, width):
    n, rows, _ = q.shape
    nblk = max(blocks) + 1
    sel = jnp.asarray([[1.0 if blocks[r] == b else 0.0 for b in range(nblk)] for r in range(rows)], q.dtype)
    return (q[:, :, None, :] * sel[None, :, :, None]).reshape(n, rows, nblk * width)


def kernel(x_prompt, x_sample, c_prompt, c_sample, cache_moba_k, cache_moba_v, cache_mla_ckv, cache_mla_kpe,
           cache_diff_k, cache_diff_v, page_table, mod_w, mod_b, norm_mix, norm_ffn, ab_w_in, mla_q_norm,
           mla_kv_norm, mla_w_uq, mla_w_uk, mla_w_uv, ab_w_out, c_w_in, diff_lambda, diff_subln, c_w_out,
           moe_w_group, moe_b_group, moe_w_expert, moe_b_expert, moe_w1, moe_w3, moe_w2, final_norm):
    batch, seq, d = x_prompt.shape
    n_seq = x_sample.shape[0]
    depth = mod_w.shape[0]
    n_pool = cache_moba_k.shape[1]
    n_pages = page_table.shape[1]
    past_len = n_pages * cache_moba_k.shape[2]
    nb = seq // MOBA_BLOCK
    assert d == D_MODEL and x_sample.shape[1] == 1 and cache_moba_k.shape[2] == LANE
    assert seq % MOBA_BLOCK == 0 and past_len % MOBA_BLOCK == 0 and n_seq % 8 == 0
    assert nb <= SEL_LANES and seq <= 256 * 256
    n_p = batch * seq
    tpb = seq // TOKEN_TILE

    xp = x_prompt.reshape(n_p, d)
    xs = x_sample.reshape(n_seq, d)
    mod = _modulation(jnp.concatenate([c_prompt, c_sample], axis=0), mod_w, mod_b)
    mod = mod.reshape(depth, batch + n_seq, 6, d)
    pt_flat = page_table.reshape(-1)

    n_all = cache_moba_k.shape[0] * n_pool
    mk_pages = jnp.transpose(cache_moba_k, (0, 1, 3, 4, 2)).reshape(n_all, 128, LANE)
    mv_pages = jnp.transpose(cache_moba_v, (0, 1, 3, 4, 2)).reshape(n_all, 128, LANE)
    ckv_pages = cache_mla_ckv.reshape(n_all, LANE, 128)
    kpe_pages = jnp.transpose(cache_mla_kpe, (0, 1, 3, 2)).reshape(n_all, MLA_ROPE_DIM, LANE)
    n_all_c = cache_diff_k.shape[0] * n_pool
    dk_pages = jnp.transpose(cache_diff_k, (0, 1, 3, 4, 5, 2)).reshape(n_all_c, 256, LANE)
    dv_pages = cache_diff_v.reshape(n_all_c, 2 * LANE, 128)

    cos_p, sin_p = _rope_tables(jnp.arange(seq, dtype=I32))
    cos_s, sin_s = _rope_tables(jnp.full((1,), past_len, I32))
    ktab_ab, ktab_c = _key_tables(seq)
    ktab_ab_s = jnp.zeros((1, ktab_ab.shape[1]), F32)
    ktab_c_s = jnp.zeros((1, ktab_c.shape[1]), F32)
    qconst_ab = _query_consts(_alibi_slopes(MOBA_HEADS), 1)
    qconst_c = _query_consts(_alibi_slopes(DIFF_HEADS), 2)
    kones = _lane_table(1, LANE, {MLA_ONES_LANE - MLA_KV_RANK: 1.0})

    n_exp_all = depth * N_EXPERTS
    w1_all = moe_w1.reshape(n_exp_all, d, EXPERT_FF)
    w3_all = moe_w3.reshape(n_exp_all, d, EXPERT_FF)
    w2_all = moe_w2.reshape(n_exp_all, EXPERT_FF, d)
    n_slots = TOP_EXPERTS * (n_p + n_seq)
    n_tiles_max = -(-n_slots // EXPERT_TILE) + N_EXPERTS

    moba_blocks = [h // (MOBA_HEADS // MOBA_KV_HEADS) for h in range(MOBA_HEADS)]
    diff_rows = [(g, c, j) for g in range(DIFF_KV_HEADS) for c in range(2) for j in range(4)]

    new_p = {k: [] for k in ("mk", "mv", "ckv", "kpe", "dk", "dv")}
    new_s = {k: [] for k in ("mk", "mv", "ckv", "kpe", "dk", "dv")}
    for l in range(depth):
        i = l // 2
        mp = [mod[l, :batch, k].reshape(batch, 1, d) for k in range(6)]
        ms = [mod[l, batch:, k] for k in range(6)]
        g_mix = norm_mix[l].reshape(1, d)
        if l % 2 == 0:
            w1, wuq, wukt, wuv_bd = _ab_weights(ab_w_in[i], mla_w_uq[i], mla_w_uk[i], mla_w_uv[i])
            wout = ab_w_out[i].astype(BF16)
            qg = mla_q_norm[i].reshape(1, -1)
            kvg = mla_kv_norm[i].reshape(1, -1)
            qt, mk, mv, ka, vt, kmean, qft, ckv, kpe, kfull, kft = _inproj_ab(
                xp, mp[0], mp[1], g_mix, w1, qg, kvg, wuq, wukt, cos_p, sin_p, qconst_ab, ktab_ab, kones,
                False, TOKEN_TILE, tpb)
            km = kmean.reshape(batch, tpb, MOBA_KV_HEADS, 64).transpose(0, 2, 1, 3)
            km = jnp.pad(km, ((0, 0), (0, 0), (0, SEL_LANES - tpb), (0, LANE - 64)))
            oa = _moba_prompt(qt, ka, vt, km, batch, seq)
            olat = _mla_prompt(qft, kfull, kft, batch, seq)
            xp = _outproj_ab(xp, mp[2], oa, olat, wuv_bd, wout, False, TOKEN_TILE, tpb)
            new_p["mk"].append(mk); new_p["mv"].append(mv); new_p["ckv"].append(ckv); new_p["kpe"].append(kpe)
            qt, mk, mv, ka, vt, kmean, qft, ckv, kpe, kfull, kft = _inproj_ab(
                xs, ms[0], ms[1], g_mix, w1, qg, kvg, wuq, wukt, cos_s, sin_s, qconst_ab, ktab_ab_s, kones,
                True, n_seq, 1)
            q_heads = jnp.transpose(qt[0].reshape(MOBA_HEADS, LANE, n_seq), (2, 0, 1))
            q_bd = _block_diag_rows(q_heads[:, :, :64], moba_blocks, 64)
            o_raw = _moba_decode(pt_flat, q_bd, mk.reshape(n_seq, 1, 128), mv.reshape(n_seq, 1, 128),
                                 mk_pages, mv_pages, i, n_pool, n_seq, n_pages)
            o_raw = o_raw.reshape(n_seq, MOBA_KV_HEADS, 4, MOBA_KV_HEADS, 64)
            oa = jnp.concatenate([o_raw[:, g, :, g, :] for g in range(MOBA_KV_HEADS)], axis=1)
            oa = oa.reshape(n_seq, 512).astype(BF16)
            qf3 = jnp.transpose(qft[0].reshape(MLA_HEADS, 256, n_seq), (2, 0, 1))
            olat = _mla_decode(pt_flat, qf3[:, :, :128], qf3[:, :, 128:128 + MLA_ROPE_DIM],
                               ckv.reshape(n_seq, 1, 128), kpe.reshape(n_seq, 1, MLA_ROPE_DIM),
                               ckv_pages, kpe_pages, i, n_pool, n_seq, n_pages)
            xs = _outproj_ab(xs, ms[2], oa, olat.reshape(n_seq, 1024), wuv_bd, wout, True, n_seq, 1)
            new_s["mk"].append(mk); new_s["mv"].append(mv); new_s["ckv"].append(ckv); new_s["kpe"].append(kpe)
        else:
            lam_init = 0.8 - 0.6 * math.exp(-0.3 * l)
            wc = _c_weights(c_w_in[i])
            wout = c_w_out[i].astype(BF16)
            lam = diff_lambda[i]
            subg = diff_subln[i].reshape(1, -1)
            qt, k, v, ka, vt = _inproj_c(xp, mp[0], mp[1], g_mix, wc, qconst_c, ktab_c, False, TOKEN_TILE, tpb)
            o = _diff_prompt(qt, ka, vt, lam, subg, lam_init, batch, seq)
            xp = _outproj_c(xp, mp[2], o, wout, False, TOKEN_TILE, tpb)
            new_p["dk"].append(k); new_p["dv"].append(v)
            qt, k, v, ka, vt = _inproj_c(xs, ms[0], ms[1], g_mix, wc, qconst_c, ktab_c_s, True, n_seq, 1)
            q4 = jnp.transpose(qt[0].reshape(DIFF_HEADS, 2, LANE, n_seq), (3, 0, 1, 2))[:, :, :, :64]
            q_rows = jnp.stack([q4[:, g * 4 + j, c, :] for (g, c, j) in diff_rows], axis=1)
            q_bd = _block_diag_rows(q_rows, [g * 2 + c for (g, c, j) in diff_rows], 64)
            o = _diff_decode(pt_flat, q_bd, k.reshape(n_seq, 1, 256), v.reshape(n_seq, 1, 256), lam, subg,
                             dk_pages, dv_pages, i, n_pool, n_seq, n_pages, lam_init)
            xs = _outproj_c(xs, ms[2], o.reshape(n_seq, 1024), wout, True, n_seq, 1)
            new_s["dk"].append(k); new_s["dv"].append(v)
        wr = jnp.concatenate([moe_w_expert[l], moe_w_group[l],
                              jnp.zeros((d, LANE - N_EXPERTS - N_GROUPS), F32)], axis=1)
        br = jnp.concatenate([moe_b_expert[l], moe_b_group[l],
                              jnp.zeros((LANE - N_EXPERTS - N_GROUPS,), F32)]).reshape(1, LANE)
        g_ffn = norm_ffn[l].reshape(1, d)
        hp_p, info_p = _router(xp, mp[3], mp[4], g_ffn, wr, br, False, TOKEN_TILE, tpb)
        hp_s, info_s = _router(xs, ms[3], ms[4], g_ffn, wr, br, True, n_seq, 1)
        ids = jnp.concatenate([info_p[:, 0:TOP_EXPERTS], info_s[:, 0:TOP_EXPERTS]], axis=0).astype(I32).reshape(-1)
        dest, tile_expert, n_tiles = _routing_offsets(ids, n_tiles_max)
        dest_p = dest[:TOP_EXPERTS * n_p].reshape(n_p // TOKEN_TILE, 1, TOP_EXPERTS * TOKEN_TILE)
        dest_s = dest[TOP_EXPERTS * n_p:].reshape(1, 1, TOP_EXPERTS * n_seq)
        x_sorted = jnp.zeros((n_tiles_max * EXPERT_TILE, 512), U32)
        x_sorted = _dispatch(dest_p, hp_p, x_sorted, TOKEN_TILE)
        x_sorted = _dispatch(dest_s, hp_s, x_sorted, n_seq)
        y_sorted = _experts(tile_expert + l * N_EXPERTS, n_tiles, x_sorted, w1_all, w3_all, w2_all)
        xp = _combine(dest_p, info_p, xp, mp[5], y_sorted, False, TOKEN_TILE, tpb)
        xs = _combine(dest_s, info_s, xs, ms[5], y_sorted, True, n_seq, 1)

    fg = final_norm.reshape(1, d)
    y_prompt = _final_norm(xp, fg, TOKEN_TILE).reshape(batch, seq, d)
    y_sample = _final_norm(xs, fg, n_seq).reshape(n_seq, 1, d)

    def stack(lst, shape):
        return jnp.stack([a.reshape(shape) for a in lst])

    outs_p = (stack(new_p["mk"], (batch, seq, MOBA_KV_HEADS, MOBA_HEAD_DIM)),
              stack(new_p["mv"], (batch, seq, MOBA_KV_HEADS, MOBA_HEAD_DIM)),
              stack(new_p["ckv"], (batch, seq, MLA_KV_RANK)),
              stack(new_p["kpe"], (batch, seq, MLA_ROPE_DIM)),
              stack(new_p["dk"], (batch, seq, DIFF_KV_HEADS, 2, DIFF_HEAD_DIM)),
              stack(new_p["dv"], (batch, seq, DIFF_KV_HEADS, 2 * DIFF_HEAD_DIM)))
    outs_s = (stack(new_s["mk"], (n_seq, 1, MOBA_KV_HEADS, MOBA_HEAD_DIM)),
              stack(new_s["mv"], (n_seq, 1, MOBA_KV_HEADS, MOBA_HEAD_DIM)),
              stack(new_s["ckv"], (n_seq, 1, MLA_KV_RANK)),
              stack(new_s["kpe"], (n_seq, 1, MLA_ROPE_DIM)),
              stack(new_s["dk"], (n_seq, 1, DIFF_KV_HEADS, 2, DIFF_HEAD_DIM)),
              stack(new_s["dv"], (n_seq, 1, DIFF_KV_HEADS, 2 * DIFF_HEAD_DIM)))
    return (y_prompt, y_sample) + outs_p + outs_s
```

```python
import functools
import math

import jax
import jax.numpy as jnp
from jax import lax
from jax.experimental import pallas as pl
from jax.experimental.pallas import tpu as pltpu

F32 = jnp.float32
BF16 = jnp.bfloat16
U32 = jnp.uint32
I32 = jnp.int32
HIGHEST = lax.Precision.HIGHEST

D_MODEL = 1024
MOBA_HEADS, MOBA_KV_HEADS, MOBA_HEAD_DIM, MOBA_BLOCK, MOBA_TOPK = 8, 2, 64, 256, 3
MLA_HEADS, MLA_Q_RANK, MLA_KV_RANK, MLA_NOPE_DIM, MLA_ROPE_DIM, MLA_V_DIM = 8, 256, 128, 64, 32, 64
ROPE_BASE = 10000.0
DIFF_HEADS, DIFF_KV_HEADS, DIFF_HEAD_DIM = 8, 2, 64
N_GROUPS, EXPERTS_PER_GROUP, EXPERT_FF, TOP_EXPERTS = 4, 8, 256, 2
N_EXPERTS = N_GROUPS * EXPERTS_PER_GROUP
EPS = 1e-6
NEG = -1e30

LANE = 128
VMEM_LIMIT = 56 * 1024 * 1024
TOKEN_TILE = 256
EXPERT_TILE = 256
SEL_LANE0, SEL_LANES = 64, 16
POS_HI_LANE, POS_LO_LANE = 80, 81
ONES_LANE = 64
MOBA_V_ROWS = 80
DIFF_V_ROWS = 144
MLA_V_ROWS = 176
MLA_ONES_LANE = MLA_KV_RANK + MLA_ROPE_DIM
AB_IN_EXT = 1024 + 256 + 512 + 256 + 128 + 128 + 128
UQ_EXT = 512 + 1024 + 1024
C_IN_EXT = 2048 + 256 + 256 + 512 + 512


def _cparams(sem):
    return pltpu.CompilerParams(dimension_semantics=sem, vmem_limit_bytes=VMEM_LIMIT)


def _alibi_slopes(n):
    return [2.0 ** (-8.0 * (i + 1) / n) for i in range(n)]


def _slope_column(values):
    row = lax.broadcasted_iota(I32, (len(values), 1), 0)
    col = jnp.zeros((len(values), 1), F32)
    for r, v in enumerate(values):
        col = jnp.where(row == r, v, col)
    return col


def _rms(x, g):
    return x * lax.rsqrt(jnp.mean(x * x, axis=-1, keepdims=True) + EPS) * g


def _ada(x, g, shift, scale):
    return _rms(x, g) * (1.0 + scale) + shift


def _mod_kernel(c_ref, w_ref, b_ref, o_ref):
    c = c_ref[...]
    a = c / (1.0 + jnp.exp(-c))
    o_ref[0] = jnp.dot(a, w_ref[0], preferred_element_type=F32, precision=HIGHEST) + b_ref[0]


def _modulation(c_all, mod_w, mod_b):
    depth, d, n6 = mod_w.shape
    rows = c_all.shape[0]
    tn = 1024
    return pl.pallas_call(
        _mod_kernel,
        grid=(depth, n6 // tn),
        in_specs=[
            pl.BlockSpec((rows, d), lambda l, j: (0, 0)),
            pl.BlockSpec((1, d, tn), lambda l, j: (l, 0, j)),
            pl.BlockSpec((1, 1, tn), lambda l, j: (l, 0, j)),
        ],
        out_specs=pl.BlockSpec((1, rows, tn), lambda l, j: (l, 0, j)),
        out_shape=jax.ShapeDtypeStruct((depth, rows, n6), F32),
        compiler_params=_cparams(("arbitrary", "arbitrary")),
        name="modulation",
    )(c_all, mod_w, mod_b.reshape(depth, 1, n6))


def _mod_spec(per_row, tm, tiles_per_batch):
    if per_row:
        return pl.BlockSpec((tm, D_MODEL), lambda t: (t, 0))
    return pl.BlockSpec((None, 1, D_MODEL), lambda t: (t // tiles_per_batch, 0, 0))


def _pos_spec(per_row, tm, width, tiles_per_batch):
    if per_row:
        return pl.BlockSpec((1, width), lambda t: (0, 0))
    return pl.BlockSpec((tm, width), lambda t: (t % tiles_per_batch, 0))


def _full_spec(shape):
    nd = len(shape)
    return pl.BlockSpec(shape, lambda t: (0,) * nd)


def _inproj_ab_kernel(x_ref, sh_ref, sc_ref, g_ref, w1_ref, qg_ref, kvg_ref, wuq_ref, wukt_ref, cos_ref, sin_ref,
                      qconst_ref, ktab_ref, kones_ref,
                      qt_ref, mk_ref, mv_ref, ka_ref, vt_ref, kmean_ref, qft_ref, ckv_ref, kpe_ref, kfull_ref, kft_ref):
    h = _ada(x_ref[...], g_ref[...], sh_ref[...], sc_ref[...]).astype(BF16)
    z = jnp.dot(h, w1_ref[...], preferred_element_type=F32)
    qt_ref[0] = (z[:, 0:1024] * (MOBA_HEAD_DIM ** -0.5) + qconst_ref[...]).T.astype(BF16)
    mk = z[:, 1024:1152]
    mk_ref[...] = mk
    mv_ref[...] = z[:, 1152:1280]
    ka_ref[...] = (z[:, 1280:1536] + ktab_ref[:, 0:256]).astype(BF16)
    vt_ref[0] = (z[:, 1536:1792] + ktab_ref[:, 256:512]).T.astype(BF16)
    kmean_ref[0] = jnp.mean(mk, axis=0, keepdims=True)
    cos = cos_ref[...]
    sin = sin_ref[...]
    ckv = _rms(z[:, 2048:2176], kvg_ref[...])
    ckv_ref[...] = ckv
    kpe = z[:, 2176:2304] * cos + z[:, 2304:2432] * sin
    kpe_ref[...] = kpe[:, 0:MLA_ROPE_DIM]
    kpe1 = kpe + kones_ref[...]
    kfull_ref[:, 0:128] = ckv.astype(BF16)
    kfull_ref[:, 128:256] = kpe1.astype(BF16)
    kft_ref[0, 0:128, :] = ckv.T.astype(BF16)
    kft_ref[0, 128:256, :] = kpe1.T.astype(BF16)
    cqn = _rms(z[:, 1792:2048], qg_ref[...]).astype(BF16)
    q2 = jnp.dot(cqn, wuq_ref[...], preferred_element_type=F32)
    scale = (MLA_NOPE_DIM + MLA_ROPE_DIM) ** -0.5
    for hd in range(MLA_HEADS):
        nope = q2[:, hd * 64:(hd + 1) * 64].astype(BF16)
        qlat = jnp.dot(nope, wukt_ref[hd], preferred_element_type=F32)
        qft_ref[0, hd * 256:hd * 256 + 128, :] = (qlat * scale).T.astype(BF16)
        pe = q2[:, 512 + hd * 128:512 + (hd + 1) * 128] * cos + q2[:, 1536 + hd * 128:1536 + (hd + 1) * 128] * sin
        qft_ref[0, hd * 256 + 128:(hd + 1) * 256, :] = (pe * scale).T.astype(BF16)


def _inproj_ab(x, shift, scale, g, w1, qg, kvg, wuq, wukt, cos_t, sin_t, qconst, ktab, kones,
               per_row, tm, tiles_per_batch):
    n = x.shape[0]
    nt = n // tm
    row = lambda w: pl.BlockSpec((tm, w), lambda t: (t, 0))
    pos = lambda w: _pos_spec(per_row, tm, w, tiles_per_batch)
    tposed = lambda feats: (jax.ShapeDtypeStruct((nt, feats, tm), BF16),
                            pl.BlockSpec((1, feats, tm), lambda t: (t, 0, 0)))
    outs = [
        tposed(1024),
        (jax.ShapeDtypeStruct((n, 128), F32), row(128)),
        (jax.ShapeDtypeStruct((n, 128), F32), row(128)),
        (jax.ShapeDtypeStruct((n, 256), BF16), row(256)),
        tposed(256),
        (jax.ShapeDtypeStruct((nt, 1, 128), F32), pl.BlockSpec((1, 1, 128), lambda t: (t, 0, 0))),
        tposed(MLA_HEADS * 256),
        (jax.ShapeDtypeStruct((n, 128), F32), row(128)),
        (jax.ShapeDtypeStruct((n, MLA_ROPE_DIM), F32), row(MLA_ROPE_DIM)),
        (jax.ShapeDtypeStruct((n, 256), BF16), row(256)),
        tposed(256),
    ]
    return pl.pallas_call(
        _inproj_ab_kernel,
        grid=(nt,),
        in_specs=[
            row(D_MODEL), _mod_spec(per_row, tm, tiles_per_batch), _mod_spec(per_row, tm, tiles_per_batch),
            _full_spec((1, D_MODEL)), _full_spec((D_MODEL, AB_IN_EXT)), _full_spec((1, MLA_Q_RANK)),
            _full_spec((1, MLA_KV_RANK)), _full_spec((MLA_Q_RANK, UQ_EXT)), _full_spec((MLA_HEADS, 64, 128)),
            pos(LANE), pos(LANE), _full_spec((1, 1024)), pos(512), _full_spec((1, LANE)),
        ],
        out_specs=[o[1] for o in outs],
        out_shape=[o[0] for o in outs],
        compiler_params=_cparams(("arbitrary",)),
        name="inproj_ab",
    )(x, shift, scale, g, w1, qg, kvg, wuq, wukt, cos_t, sin_t, qconst, ktab, kones)


def _inproj_c_kernel(x_ref, sh_ref, sc_ref, g_ref, w_ref, qconst_ref, ktab_ref, qt_ref, k_ref, v_ref, ka_ref, vt_ref):
    h = _ada(x_ref[...], g_ref[...], sh_ref[...], sc_ref[...]).astype(BF16)
    z = jnp.dot(h, w_ref[...], preferred_element_type=F32)
    qt_ref[0] = (z[:, 0:2048] * (DIFF_HEAD_DIM ** -0.5) + qconst_ref[...]).T.astype(BF16)
    k_ref[...] = z[:, 2048:2304]
    v_ref[...] = z[:, 2304:2560]
    ka_ref[...] = (z[:, 2560:3072] + ktab_ref[:, 0:512]).astype(BF16)
    vt_ref[0] = (z[:, 3072:3584] + ktab_ref[:, 512:1024]).T.astype(BF16)


def _inproj_c(x, shift, scale, g, w, qconst, ktab, per_row, tm, tiles_per_batch):
    n = x.shape[0]
    nt = n // tm
    row = lambda wd: pl.BlockSpec((tm, wd), lambda t: (t, 0))
    tspec = lambda feats: pl.BlockSpec((1, feats, tm), lambda t: (t, 0, 0))
    return pl.pallas_call(
        _inproj_c_kernel,
        grid=(nt,),
        in_specs=[row(D_MODEL), _mod_spec(per_row, tm, tiles_per_batch), _mod_spec(per_row, tm, tiles_per_batch),
                  _full_spec((1, D_MODEL)), _full_spec((D_MODEL, C_IN_EXT)), _full_spec((1, 2048)),
                  _pos_spec(per_row, tm, 1024, tiles_per_batch)],
        out_specs=[tspec(2048), row(256), row(256), row(512), tspec(512)],
        out_shape=[jax.ShapeDtypeStruct((nt, 2048, tm), BF16), jax.ShapeDtypeStruct((n, 256), F32),
                   jax.ShapeDtypeStruct((n, 256), F32), jax.ShapeDtypeStruct((n, 512), BF16),
                   jax.ShapeDtypeStruct((nt, 512, tm), BF16)],
        compiler_params=_cparams(("arbitrary",)),
        name="inproj_c",
    )(x, shift, scale, g, w, qconst, ktab)


def _outproj_ab_kernel(x_ref, gate_ref, oa_ref, olat_ref, wuv_ref, wout_ref, o_ref):
    ob = jnp.dot(olat_ref[...], wuv_ref[...], preferred_element_type=F32).astype(BF16)
    y = jnp.dot(oa_ref[...], wout_ref[0:512, :], preferred_element_type=F32)
    y = y + jnp.dot(ob, wout_ref[512:1024, :], preferred_element_type=F32)
    o_ref[...] = x_ref[...] + gate_ref[...] * y


def _outproj_ab(x, gate, oa, olat, wuv_bd, wout, per_row, tm, tiles_per_batch):
    n = x.shape[0]
    row = lambda wd: pl.BlockSpec((tm, wd), lambda t: (t, 0))
    return pl.pallas_call(
        _outproj_ab_kernel,
        grid=(n // tm,),
        in_specs=[row(D_MODEL), _mod_spec(per_row, tm, tiles_per_batch), row(512), row(1024),
                  _full_spec((1024, 512)), _full_spec((1024, D_MODEL))],
        out_specs=row(D_MODEL),
        out_shape=jax.ShapeDtypeStruct((n, D_MODEL), F32),
        compiler_params=_cparams(("arbitrary",)),
        name="outproj_ab",
    )(x, gate, oa, olat, wuv_bd, wout)


def _outproj_c_kernel(x_ref, gate_ref, o_in_ref, wout_ref, o_ref):
    y = jnp.dot(o_in_ref[...], wout_ref[...], preferred_element_type=F32)
    o_ref[...] = x_ref[...] + gate_ref[...] * y


def _outproj_c(x, gate, o_in, wout, per_row, tm, tiles_per_batch):
    n = x.shape[0]
    row = lambda wd: pl.BlockSpec((tm, wd), lambda t: (t, 0))
    return pl.pallas_call(
        _outproj_c_kernel,
        grid=(n // tm,),
        in_specs=[row(D_MODEL), _mod_spec(per_row, tm, tiles_per_batch), row(1024), _full_spec((1024, D_MODEL))],
        out_specs=row(D_MODEL),
        out_shape=jax.ShapeDtypeStruct((n, D_MODEL), F32),
        compiler_params=_cparams(("arbitrary",)),
        name="outproj_c",
    )(x, gate, o_in, wout)


def _router_kernel(x_ref, sh_ref, sc_ref, g_ref, wr_ref, br_ref, hp_ref, info_ref):
    h = _ada(x_ref[...], g_ref[...], sh_ref[...], sc_ref[...])
    h_hi = h.astype(BF16)
    bits = lax.bitcast_convert_type(h_hi.astype(F32), U32)
    hp_ref[...] = bits[:, 0:512] | (bits[:, 512:1024] >> 16)
    h_lo = (h - h_hi.astype(F32)).astype(BF16)
    logit = (jnp.dot(h_hi, wr_ref[0], preferred_element_type=F32)
             + jnp.dot(h_lo, wr_ref[0], preferred_element_type=F32)
             + jnp.dot(h_hi, wr_ref[1], preferred_element_type=F32)) + br_ref[...]
    lane = lax.broadcasted_iota(I32, logit.shape, 1)
    is_grp = (lane >= N_EXPERTS) & (lane < N_EXPERTS + N_GROUPS)
    lg = jnp.where(is_grp, logit, -jnp.inf)
    mg = jnp.max(lg, axis=-1, keepdims=True)
    g_w = 1.0 / jnp.sum(jnp.exp(lg - mg), axis=-1, keepdims=True)
    g_idx = jnp.min(jnp.where(lg == mg, lane, 4 * LANE), axis=-1, keepdims=True) - N_EXPERTS
    in_grp = (lane < N_EXPERTS) & ((lane // EXPERTS_PER_GROUP) == g_idx)
    le = jnp.where(in_grp, logit, -jnp.inf)
    m1 = jnp.max(le, axis=-1, keepdims=True)
    i1 = jnp.min(jnp.where(le == m1, lane, 4 * LANE), axis=-1, keepdims=True)
    le2 = jnp.where(lane == i1, -jnp.inf, le)
    m2 = jnp.max(le2, axis=-1, keepdims=True)
    i2 = jnp.min(jnp.where(le2 == m2, lane, 4 * LANE), axis=-1, keepdims=True)
    r = jnp.exp(m2 - m1)
    w1 = g_w / (1.0 + r)
    w2 = g_w * r / (1.0 + r)
    info = jnp.where(lane == 0, i1.astype(F32), 0.0) + jnp.where(lane == 1, i2.astype(F32), 0.0)
    info_ref[...] = info + jnp.where(lane == 2, w1, 0.0) + jnp.where(lane == 3, w2, 0.0)


def _router(x, shift, scale, g, wr, br, per_row, tm, tiles_per_batch):
    n = x.shape[0]
    row = lambda wd: pl.BlockSpec((tm, wd), lambda t: (t, 0))
    return pl.pallas_call(
        _router_kernel,
        grid=(n // tm,),
        in_specs=[row(D_MODEL), _mod_spec(per_row, tm, tiles_per_batch), _mod_spec(per_row, tm, tiles_per_batch),
                  _full_spec((1, D_MODEL)), _full_spec((2, D_MODEL, LANE)), _full_spec((1, LANE))],
        out_specs=[row(512), row(LANE)],
        out_shape=[jax.ShapeDtypeStruct((n, 512), U32), jax.ShapeDtypeStruct((n, LANE), F32)],
        compiler_params=_cparams(("arbitrary",)),
        name="router",
    )(x, shift, scale, g, wr, br)


def _routing_offsets(expert_ids, n_tiles_max):
    onehot = (expert_ids[:, None] == jnp.arange(N_EXPERTS, dtype=I32)[None, :]).astype(I32)
    csum = jnp.cumsum(onehot, axis=0)
    counts = csum[-1]
    rank = jnp.sum(onehot * csum, axis=1) - 1
    padded = ((counts + EXPERT_TILE - 1) // EXPERT_TILE) * EXPERT_TILE
    ends = jnp.cumsum(padded)
    starts = ends - padded
    dest = jnp.sum(onehot * starts[None, :], axis=1) + rank
    tile_rows = jnp.arange(n_tiles_max, dtype=I32) * EXPERT_TILE
    tile_expert = jnp.minimum(jnp.sum((tile_rows[:, None] >= ends[None, :]).astype(I32), axis=1), N_EXPERTS - 1)
    n_tiles = (ends[-1] // EXPERT_TILE).reshape(1)
    return dest.astype(I32), tile_expert.astype(I32), n_tiles.astype(I32)


def _dispatch_kernel(dest_ref, hp_ref, xs_in_ref, xs_ref, sem, *, tm):
    del xs_in_ref

    for r in range(tm):
        for k in range(TOP_EXPERTS):
            d = dest_ref[0, 0, TOP_EXPERTS * r + k]
            pltpu.make_async_copy(hp_ref.at[pl.ds(r, 1)], xs_ref.at[pl.ds(d, 1)], sem.at[0]).start(priority=k % 2)
    for k in range(TOP_EXPERTS):
        pltpu.make_async_copy(hp_ref, xs_ref.at[pl.ds(0, tm)], sem.at[0]).wait()


def _dispatch(dest_tiles, hp, xs, tm):
    n = hp.shape[0]
    return pl.pallas_call(
        functools.partial(_dispatch_kernel, tm=tm),
        grid=(n // tm,),
        in_specs=[pl.BlockSpec((1, 1, TOP_EXPERTS * tm), lambda t: (t, 0, 0), memory_space=pltpu.SMEM),
                  pl.BlockSpec((tm, 512), lambda t: (t, 0)),
                  pl.BlockSpec(memory_space=pl.ANY)],
        out_specs=pl.BlockSpec(memory_space=pl.ANY),
        out_shape=jax.ShapeDtypeStruct(xs.shape, U32),
        scratch_shapes=[pltpu.SemaphoreType.DMA((1,))],
        input_output_aliases={2: 0},
        compiler_params=_cparams(("arbitrary",)),
        name="moe_dispatch",
    )(dest_tiles, hp, xs)


def _experts_kernel(te_ref, nt_ref, xs_ref, w1_ref, w3_ref, w2_ref, y_ref):
    del te_ref

    @pl.when(pl.program_id(0) < nt_ref[0])
    def _():
        packed = xs_ref[...]
        hi = lax.bitcast_convert_type(packed & jnp.uint32(0xFFFF0000), F32).astype(BF16)
        lo = lax.bitcast_convert_type(packed << 16, F32).astype(BF16)
        x = jnp.concatenate([hi, lo], axis=1)
        a = jnp.dot(x, w1_ref[0].astype(BF16), preferred_element_type=F32)
        b = jnp.dot(x, w3_ref[0].astype(BF16), preferred_element_type=F32)
        hid = (a / (1.0 + jnp.exp(-a))) * b
        y_ref[...] = jnp.dot(hid.astype(BF16), w2_ref[0].astype(BF16), preferred_element_type=F32)

    @pl.when(pl.program_id(0) >= nt_ref[0])
    def _():
        y_ref[...] = jnp.zeros_like(y_ref)


def _experts(tile_expert, n_tiles, xs, w1, w3, w2):
    n_tiles_max = xs.shape[0] // EXPERT_TILE
    live = lambda t, te, nt: jnp.minimum(t, nt[0] - 1)
    wspec = lambda a, b: pl.BlockSpec((1, a, b), lambda t, te, nt: (te[live(t, te, nt)], 0, 0))
    return pl.pallas_call(
        _experts_kernel,
        grid_spec=pltpu.PrefetchScalarGridSpec(
            num_scalar_prefetch=2,
            grid=(n_tiles_max,),
            in_specs=[pl.BlockSpec((EXPERT_TILE, 512), lambda t, te, nt: (live(t, te, nt), 0)),
                      wspec(D_MODEL, EXPERT_FF), wspec(D_MODEL, EXPERT_FF), wspec(EXPERT_FF, D_MODEL)],
            out_specs=pl.BlockSpec((EXPERT_TILE, D_MODEL), lambda t, te, nt: (t, 0))),
        out_shape=jax.ShapeDtypeStruct((xs.shape[0], D_MODEL), F32),
        compiler_params=_cparams(("arbitrary",)),
        name="moe_experts",
    )(tile_expert, n_tiles, xs, w1, w3, w2)


def _combine_kernel(dest_ref, dest_next_ref, info_ref, x_ref, gmod_ref, fg_ref, y_hbm, o_ref, ybuf, sem, *,
                    tm, final):
    t = pl.program_id(0)
    nt = pl.num_programs(0)
    slot = t % 2

    def start_all(dref, slot_):
        for r in range(tm):
            for k in range(TOP_EXPERTS):
                d = dref[0, 0, TOP_EXPERTS * r + k]
                pltpu.make_async_copy(y_hbm.at[pl.ds(d, 1)], ybuf.at[slot_, pl.ds(k * tm + r, 1)],
                                      sem.at[slot_]).start(priority=k % 2)

    @pl.when(t == 0)
    def _():
        start_all(dest_ref, 0)

    @pl.when(t + 1 < nt)
    def _():
        start_all(dest_next_ref, 1 - slot)

    pltpu.make_async_copy(y_hbm.at[pl.ds(0, TOP_EXPERTS * tm)], ybuf.at[slot], sem.at[slot]).wait()
    info = info_ref[...]
    y = info[:, 2:3] * ybuf[slot, 0:tm] + info[:, 3:4] * ybuf[slot, tm:2 * tm]
    x_new = x_ref[...] + gmod_ref[...] * y
    o_ref[...] = _rms(x_new, fg_ref[...]) if final else x_new


def _combine(dest_tiles, info, x, gmod, fg, y_sorted, per_row, tm, tiles_per_batch, final):
    n = x.shape[0]
    nt = n // tm
    row = lambda wd: pl.BlockSpec((tm, wd), lambda t: (t, 0))
    dspec = lambda f: pl.BlockSpec((1, 1, TOP_EXPERTS * tm), lambda t: (f(t), 0, 0), memory_space=pltpu.SMEM)
    return pl.pallas_call(
        functools.partial(_combine_kernel, tm=tm, final=final),
        grid=(nt,),
        in_specs=[dspec(lambda t: t), dspec(lambda t: jnp.minimum(t + 1, nt - 1)), row(LANE), row(D_MODEL),
                  _mod_spec(per_row, tm, tiles_per_batch), _full_spec((1, D_MODEL)),
                  pl.BlockSpec(memory_space=pl.ANY)],
        out_specs=row(D_MODEL),
        out_shape=jax.ShapeDtypeStruct((n, D_MODEL), F32),
        scratch_shapes=[pltpu.VMEM((2, TOP_EXPERTS * tm, D_MODEL), F32), pltpu.SemaphoreType.DMA((2,))],
        compiler_params=_cparams(("arbitrary",)),
        name="moe_combine",
    )(dest_tiles, dest_tiles, info, x, gmod, fg, y_sorted)


def _qk(q, k):
    return lax.dot_general(q, k, (((1,), (1,)), ((), ())), preferred_element_type=F32)


def _causal_bias_t(t, copies):
    k = lax.broadcasted_iota(I32, (t, t), 0)
    q = lax.broadcasted_iota(I32, (t, t), 1)
    return jnp.concatenate([jnp.where(k <= q, 0.0, NEG)] * copies, axis=1)


def _online_softmax_step_t(m_ref, acc_ref, idx, s_t, v_t):
    m_old = m_ref[idx]
    m_new = jnp.maximum(m_old, jnp.max(s_t, axis=0, keepdims=True))
    p = jnp.exp(s_t - m_new)
    acc_ref[idx] = jnp.exp(m_old - m_new) * acc_ref[idx] + jnp.dot(v_t, p.astype(BF16), preferred_element_type=F32)
    m_ref[idx] = m_new


def _moba_prompt_kernel(qt_ref, ka_ref, vt_ref, km_ref, o_ref, q_scr, m_ref, acc_ref, *, nb):
    i = pl.program_id(1)
    blk_rows = MOBA_BLOCK
    rep = MOBA_HEADS // MOBA_KV_HEADS
    cols = rep * blk_rows
    blk = lax.broadcasted_iota(I32, (SEL_LANES, cols), 0)
    for g in range(MOBA_KV_HEADS):
        q_t = jnp.concatenate([qt_ref[0, (g * rep + j) * LANE:(g * rep + j + 1) * LANE, :] for j in range(rep)],
                              axis=1)
        gate = jnp.dot(km_ref[0, g], q_t.astype(F32), preferred_element_type=F32, precision=HIGHEST)
        valid = blk < i
        gate = jnp.where(valid, gate, -jnp.inf)
        rank = jnp.zeros((SEL_LANES, cols), I32)
        for mb in range(nb):
            gm = gate[mb:mb + 1, :]
            beats = (gm > gate) | ((gm == gate) & (blk > mb))
            rank = rank + beats.astype(I32)
        keep = (valid & (rank < MOBA_TOPK)) | (blk == i)
        bias = jnp.where((blk < nb) & jnp.logical_not(keep), NEG, 0.0)
        q_scr[g] = jnp.concatenate([q_t[0:SEL_LANE0], bias.astype(BF16), q_t[SEL_LANE0 + SEL_LANES:LANE]], axis=0)
        m_ref[g] = jnp.full((1, cols), NEG, F32)
        acc_ref[g] = jnp.zeros((MOBA_V_ROWS, cols), F32)

    def step(j, g, bias_t):
        off = pl.multiple_of(j * blk_rows, blk_rows)
        s_t = jnp.dot(ka_ref[pl.ds(off, blk_rows), g * LANE:(g + 1) * LANE], q_scr[g], preferred_element_type=F32)
        if bias_t is not None:
            s_t = s_t + bias_t
        _online_softmax_step_t(m_ref, acc_ref, g, s_t, vt_ref[j, g * LANE:g * LANE + MOBA_V_ROWS, :])

    def past(j, c):
        for g in range(MOBA_KV_HEADS):
            step(j, g, None)
        return c

    lax.fori_loop(0, i, past, 0)
    causal_t = _causal_bias_t(blk_rows, rep)
    for g in range(MOBA_KV_HEADS):
        step(i, g, causal_t)
        acc = acc_ref[g]
        o_t = acc[0:MOBA_HEAD_DIM] / acc[ONES_LANE:ONES_LANE + 1]
        o_t = jnp.concatenate([o_t, jnp.zeros_like(o_t)], axis=0)
        for j in range(rep):
            hd = g * rep + j
            o_ref[:, hd * 64:(hd + 1) * 64] = o_t[:, j * blk_rows:(j + 1) * blk_rows].T[:, 0:64].astype(BF16)


def _moba_prompt(qt, ka, vt, km, batch, seq):
    nb = seq // MOBA_BLOCK
    cols = (MOBA_HEADS // MOBA_KV_HEADS) * MOBA_BLOCK
    return pl.pallas_call(
        functools.partial(_moba_prompt_kernel, nb=nb),
        grid=(batch, nb),
        in_specs=[pl.BlockSpec((1, 1024, MOBA_BLOCK), lambda b, i: (b * nb + i, 0, 0)),
                  pl.BlockSpec((seq, 256), lambda b, i: (b, 0)),
                  pl.BlockSpec((nb, 256, MOBA_BLOCK), lambda b, i: (b, 0, 0)),
                  pl.BlockSpec((1, MOBA_KV_HEADS, SEL_LANES, LANE), lambda b, i: (b, 0, 0, 0))],
        out_specs=pl.BlockSpec((MOBA_BLOCK, 512), lambda b, i: (b * nb + i, 0)),
        out_shape=jax.ShapeDtypeStruct((batch * seq, 512), BF16),
        scratch_shapes=[pltpu.VMEM((MOBA_KV_HEADS, LANE, cols), BF16), pltpu.VMEM((MOBA_KV_HEADS, 1, cols), F32),
                        pltpu.VMEM((MOBA_KV_HEADS, MOBA_V_ROWS, cols), F32)],
        compiler_params=_cparams(("arbitrary", "arbitrary")),
        name="moba_prompt",
    )(qt, ka, vt, km)


def _mla_prompt_kernel(qt_ref, k_ref, kt_ref, o_ref, q_scr, m_ref, acc_ref):
    i = pl.program_id(1)
    tq = TOKEN_TILE
    cols = MLA_HEADS * tq
    q_scr[...] = jnp.concatenate([qt_ref[0, hd * 256:(hd + 1) * 256, :] for hd in range(MLA_HEADS)], axis=1)
    m_ref[0] = jnp.full((1, cols), NEG, F32)
    acc_ref[0] = jnp.zeros((MLA_V_ROWS, cols), F32)

    def step(j, bias_t):
        s_t = jnp.dot(k_ref[pl.ds(pl.multiple_of(j * tq, tq), tq), :], q_scr[...], preferred_element_type=F32)
        if bias_t is not None:
            s_t = s_t + bias_t
        _online_softmax_step_t(m_ref, acc_ref, 0, s_t, kt_ref[j, 0:MLA_V_ROWS, :])

    def past(j, c):
        step(j, None)
        return c

    lax.fori_loop(0, i, past, 0)
    step(i, _causal_bias_t(tq, MLA_HEADS))
    acc = acc_ref[0]
    o_t = acc[0:MLA_KV_RANK] / acc[MLA_ONES_LANE:MLA_ONES_LANE + 1]
    for hd in range(MLA_HEADS):
        o_ref[:, hd * 128:(hd + 1) * 128] = o_t[:, hd * tq:(hd + 1) * tq].T.astype(BF16)


def _mla_prompt(qft, kfull, kft, batch, seq):
    nq = seq // TOKEN_TILE
    cols = MLA_HEADS * TOKEN_TILE
    return pl.pallas_call(
        _mla_prompt_kernel,
        grid=(batch, nq),
        in_specs=[pl.BlockSpec((1, MLA_HEADS * 256, TOKEN_TILE), lambda b, i: (b * nq + i, 0, 0)),
                  pl.BlockSpec((seq, 256), lambda b, i: (b, 0)),
                  pl.BlockSpec((nq, 256, TOKEN_TILE), lambda b, i: (b, 0, 0))],
        out_specs=pl.BlockSpec((TOKEN_TILE, 1024), lambda b, i: (b * nq + i, 0)),
        out_shape=jax.ShapeDtypeStruct((batch * seq, 1024), BF16),
        scratch_shapes=[pltpu.VMEM((256, cols), BF16), pltpu.VMEM((1, 1, cols), F32),
                        pltpu.VMEM((1, MLA_V_ROWS, cols), F32)],
        compiler_params=_cparams(("arbitrary", "arbitrary")),
        name="mla_prompt",
    )(qft, kfull, kft)


def _lambda_full(lam_ref, lam_init):
    lam = lam_ref[...]
    a = jnp.sum(lam[0:1] * lam[1:2], axis=-1, keepdims=True)
    b = jnp.sum(lam[2:3] * lam[3:4], axis=-1, keepdims=True)
    return jnp.exp(a) - jnp.exp(b) + lam_init


def _diff_prompt_kernel(qt_ref, ka_ref, vt_ref, lam_ref, subg_ref, o_ref, q_scr, m_ref, acc_ref, *, lam_init):
    i = pl.program_id(1)
    tq = TOKEN_TILE
    rep = DIFF_HEADS // DIFF_KV_HEADS
    cols = rep * tq
    chains = [(g, c) for g in range(DIFF_KV_HEADS) for c in range(2)]
    for n, (g, c) in enumerate(chains):
        q_scr[n] = jnp.concatenate(
            [qt_ref[0, ((g * rep + j) * 2 + c) * LANE:((g * rep + j) * 2 + c + 1) * LANE, :] for j in range(rep)],
            axis=1)
        m_ref[n] = jnp.full((1, cols), NEG, F32)
        acc_ref[n] = jnp.zeros((DIFF_V_ROWS, cols), F32)

    def step(j, n, bias_t):
        g, c = chains[n]
        off = pl.multiple_of(j * tq, tq)
        s_t = jnp.dot(ka_ref[pl.ds(off, tq), (g * 2 + c) * LANE:(g * 2 + c + 1) * LANE], q_scr[n],
                      preferred_element_type=F32)
        if bias_t is not None:
            s_t = s_t + bias_t
        _online_softmax_step_t(m_ref, acc_ref, n, s_t, vt_ref[j, g * 256:g * 256 + DIFF_V_ROWS, :])

    def past(j, cc):
        for n in range(len(chains)):
            step(j, n, None)
        return cc

    lax.fori_loop(0, i, past, 0)
    causal_t = _causal_bias_t(tq, rep)
    lam_full = _lambda_full(lam_ref, lam_init)
    outs = []
    for n in range(len(chains)):
        step(i, n, causal_t)
        acc = acc_ref[n]
        outs.append(acc[0:128] / acc[128:129])
    for g in range(DIFF_KV_HEADS):
        o_t = outs[2 * g] - lam_full * outs[2 * g + 1]
        o_t = o_t * lax.rsqrt(jnp.mean(o_t * o_t, axis=0, keepdims=True) + EPS)
        for j in range(rep):
            hd = g * rep + j
            o = o_t[:, j * tq:(j + 1) * tq].T * subg_ref[...] * (1.0 - lam_init)
            o_ref[:, hd * 128:(hd + 1) * 128] = o.astype(BF16)


def _diff_prompt(qt, ka, vt, lam, subg, lam_init, batch, seq):
    nq = seq // TOKEN_TILE
    cols = (DIFF_HEADS // DIFF_KV_HEADS) * TOKEN_TILE
    n_chain = DIFF_KV_HEADS * 2
    return pl.pallas_call(
        functools.partial(_diff_prompt_kernel, lam_init=lam_init),
        grid=(batch, nq),
        in_specs=[pl.BlockSpec((1, 2048, TOKEN_TILE), lambda b, i: (b * nq + i, 0, 0)),
                  pl.BlockSpec((seq, 512), lambda b, i: (b, 0)),
                  pl.BlockSpec((nq, 512, TOKEN_TILE), lambda b, i: (b, 0, 0)),
                  pl.BlockSpec((4, 64), lambda b, i: (0, 0)),
                  pl.BlockSpec((1, 128), lambda b, i: (0, 0))],
        out_specs=pl.BlockSpec((TOKEN_TILE, 1024), lambda b, i: (b * nq + i, 0)),
        out_shape=jax.ShapeDtypeStruct((batch * seq, 1024), BF16),
        scratch_shapes=[pltpu.VMEM((n_chain, LANE, cols), BF16), pltpu.VMEM((n_chain, 1, cols), F32),
                        pltpu.VMEM((n_chain, DIFF_V_ROWS, cols), F32)],
        compiler_params=_cparams(("arbitrary", "arbitrary")),
        name="diff_prompt",
    )(qt, ka, vt, lam, subg)


def _page_copy(src_hbm, page, dst, sem):
    return pltpu.make_async_copy(src_hbm.at[page], dst, sem)


def _paged_pipeline(pt_ref, n_pages, layer_off, streams, sem):
    s = pl.program_id(0)
    ns = pl.num_programs(0)
    slot = s % 2

    def start(seq, slot_):
        for p in range(n_pages):
            page = pt_ref[seq * n_pages + p] + layer_off
            for hbm, buf, dst_fn in streams:
                _page_copy(hbm, page, dst_fn(buf, slot_, p), sem.at[slot_]).start()

    @pl.when(s == 0)
    def _():
        start(0, 0)

    @pl.when(s + 1 < ns)
    def _():
        start(s + 1, 1 - slot)

    for p in range(n_pages):
        for hbm, buf, dst_fn in streams:
            _page_copy(hbm, 0, dst_fn(buf, slot, p), sem.at[slot]).wait()
    return slot


def _lane_window(buf, slot, p):
    return buf.at[slot, :, pl.ds(p * LANE, LANE)]


def _row_window(rows):
    return lambda buf, slot, p: buf.at[slot, pl.ds(p * rows, rows), :]


def _moba_decode_kernel(pt_ref, q_ref, knew_ref, vnew_ref, k_hbm, v_hbm, o_ref, kbuf, vbuf, sem, *,
                        n_pages, layer_off, past_len):
    slot = _paged_pipeline(pt_ref, n_pages, layer_off, [(k_hbm, kbuf, _lane_window), (v_hbm, vbuf, _lane_window)],
                           sem)
    nblk = past_len // MOBA_BLOCK
    q = q_ref[0]
    kt = kbuf[slot].astype(BF16)
    s_raw = jnp.dot(q, kt, preferred_element_type=F32)
    lane = lax.broadcasted_iota(I32, (MOBA_HEADS, LANE), 1)
    gate = jnp.full((MOBA_HEADS, LANE), -jnp.inf, F32)
    for b in range(nblk):
        gs = jnp.sum(s_raw[:, b * MOBA_BLOCK:(b + 1) * MOBA_BLOCK], axis=-1, keepdims=True)
        gate = jnp.where(lane == b, gs, gate)
    rank = jnp.zeros((MOBA_HEADS, LANE), I32)
    for mb in range(nblk):
        gm = gate[:, mb:mb + 1]
        beats = (gm > gate) | ((gm == gate) & (lane > mb))
        rank = rank + beats.astype(I32)
    sel = (lane < nblk) & (rank < MOBA_TOPK)
    selmask = jnp.concatenate(
        [jnp.broadcast_to(jnp.sum(jnp.where((lane == b) & sel, 1.0, 0.0), axis=-1, keepdims=True) > 0.5,
                          (MOBA_HEADS, MOBA_BLOCK)) for b in range(nblk)], axis=1)
    slope = _slope_column(_alibi_slopes(MOBA_HEADS))
    pos = lax.broadcasted_iota(I32, (MOBA_HEADS, past_len), 1)
    dist = (past_len - pos).astype(F32)
    s = jnp.where(selmask, s_raw - slope * dist, NEG)
    s_own = jnp.sum(q.astype(F32) * knew_ref[0], axis=-1, keepdims=True)
    m = jnp.maximum(jnp.max(s, axis=-1, keepdims=True), s_own)
    p = jnp.exp(s - m)
    p_own = jnp.exp(s_own - m)
    l = jnp.sum(p, axis=-1, keepdims=True) + p_own
    vt = vbuf[slot].astype(BF16)
    acc = _qk(p.astype(BF16), vt) + p_own * vnew_ref[0]
    o_ref[0] = acc / l


def _moba_decode(pt_flat, q_bd, knew, vnew, k_pages, v_pages, layer, n_pool, n_seq, n_pages):
    past_len = n_pages * LANE
    kern = functools.partial(_moba_decode_kernel, n_pages=n_pages, layer_off=layer * n_pool, past_len=past_len)
    return pl.pallas_call(
        kern,
        grid_spec=pltpu.PrefetchScalarGridSpec(
            num_scalar_prefetch=1,
            grid=(n_seq,),
            in_specs=[pl.BlockSpec((1, 8, 128), lambda s, pt: (s, 0, 0)),
                      pl.BlockSpec((1, 1, 128), lambda s, pt: (s, 0, 0)),
                      pl.BlockSpec((1, 1, 128), lambda s, pt: (s, 0, 0)),
                      pl.BlockSpec(memory_space=pl.ANY), pl.BlockSpec(memory_space=pl.ANY)],
            out_specs=pl.BlockSpec((1, 8, 128), lambda s, pt: (s, 0, 0)),
            scratch_shapes=[pltpu.VMEM((2, 128, past_len), F32), pltpu.VMEM((2, 128, past_len), F32),
                            pltpu.SemaphoreType.DMA((2,))]),
        out_shape=jax.ShapeDtypeStruct((n_seq, 8, 128), F32),
        compiler_params=_cparams(("arbitrary",)),
        name="moba_decode",
    )(pt_flat, q_bd, knew, vnew, k_pages, v_pages)


def _mla_decode_kernel(pt_ref, qlat_ref, qpe_ref, cnew_ref, pnew_ref, c_hbm, p_hbm, o_ref, cbuf, pbuf, sem, *,
                       n_pages, layer_off):
    slot = _paged_pipeline(pt_ref, n_pages, layer_off,
                           [(c_hbm, cbuf, _row_window(LANE)), (p_hbm, pbuf, _lane_window)], sem)
    qlat = qlat_ref[0]
    qpe = qpe_ref[0]
    ckv = cbuf[slot].astype(BF16)
    kpet = pbuf[slot].astype(BF16)
    s = _qk(qlat, ckv) + jnp.dot(qpe, kpet, preferred_element_type=F32)
    s_own = (jnp.sum(qlat.astype(F32) * cnew_ref[0], axis=-1, keepdims=True)
             + jnp.sum(qpe.astype(F32) * pnew_ref[0], axis=-1, keepdims=True))
    m = jnp.maximum(jnp.max(s, axis=-1, keepdims=True), s_own)
    p = jnp.exp(s - m)
    p_own = jnp.exp(s_own - m)
    l = jnp.sum(p, axis=-1, keepdims=True) + p_own
    acc = jnp.dot(p.astype(BF16), ckv, preferred_element_type=F32) + p_own * cnew_ref[0]
    o_ref[0] = (acc / l).astype(BF16)


def _mla_decode(pt_flat, qlat, qpe, cnew, pnew, c_pages, p_pages, layer, n_pool, n_seq, n_pages):
    past_len = n_pages * LANE
    kern = functools.partial(_mla_decode_kernel, n_pages=n_pages, layer_off=layer * n_pool)
    return pl.pallas_call(
        kern,
        grid_spec=pltpu.PrefetchScalarGridSpec(
            num_scalar_prefetch=1,
            grid=(n_seq,),
            in_specs=[pl.BlockSpec((1, 8, 128), lambda s, pt: (s, 0, 0)),
                      pl.BlockSpec((1, 8, 32), lambda s, pt: (s, 0, 0)),
                      pl.BlockSpec((1, 1, 128), lambda s, pt: (s, 0, 0)),
                      pl.BlockSpec((1, 1, 32), lambda s, pt: (s, 0, 0)),
                      pl.BlockSpec(memory_space=pl.ANY), pl.BlockSpec(memory_space=pl.ANY)],
            out_specs=pl.BlockSpec((1, 8, 128), lambda s, pt: (s, 0, 0)),
            scratch_shapes=[pltpu.VMEM((2, past_len, 128), F32), pltpu.VMEM((2, MLA_ROPE_DIM, past_len), F32),
                            pltpu.SemaphoreType.DMA((2,))]),
        out_shape=jax.ShapeDtypeStruct((n_seq, 8, 128), BF16),
        compiler_params=_cparams(("arbitrary",)),
        name="mla_decode",
    )(pt_flat, qlat, qpe, cnew, pnew, c_pages, p_pages)


def _diff_decode_kernel(pt_ref, q_ref, knew_ref, vnew_ref, lam_ref, subg_ref, k_hbm, v_hbm, o_ref,
                        kbuf, vbuf, sem, m_ref, l_ref, acc_ref, *, n_pages, pages_per_step, layer_off, lam_init):
    st = pl.program_id(0)
    halves = n_pages // pages_per_step
    half = st % halves
    slot = _paged_pipeline(pt_ref, pages_per_step, layer_off,
                           [(k_hbm, kbuf, _lane_window), (v_hbm, vbuf, _row_window(2 * LANE))], sem)
    chunk = pages_per_step * LANE
    q = q_ref[0]
    kt = kbuf[slot].astype(BF16)
    head_slopes = _alibi_slopes(DIFF_HEADS)
    slope = _slope_column([head_slopes[(r // 8) * 4 + (r % 4)] for r in range(16)])
    pos = lax.broadcasted_iota(I32, (16, chunk), 1) + half * chunk
    dist = (n_pages * LANE - pos).astype(F32)
    s = jnp.dot(q, kt, preferred_element_type=F32) - slope * dist

    @pl.when(half == 0)
    def _():
        m_ref[...] = jnp.full_like(m_ref, NEG)
        l_ref[...] = jnp.zeros_like(l_ref)
        acc_ref[...] = jnp.zeros_like(acc_ref)

    m_old = m_ref[...]
    m_new = jnp.maximum(m_old, jnp.max(s, axis=-1, keepdims=True))
    alpha = jnp.exp(m_old - m_new)
    p = jnp.exp(s - m_new)
    l_ref[...] = alpha * l_ref[...] + jnp.sum(p, axis=-1, keepdims=True)
    pb = p.astype(BF16)
    pv = []
    for g in range(DIFF_KV_HEADS):
        vg = vbuf[slot, pl.ds(g, chunk, stride=2), :].astype(BF16)
        pv.append(jnp.dot(pb[g * 8:(g + 1) * 8], vg, preferred_element_type=F32))
    acc_ref[...] = alpha * acc_ref[...] + jnp.concatenate(pv, axis=0)
    m_ref[...] = m_new

    @pl.when(half == halves - 1)
    def _():
        s_own = jnp.sum(q.astype(F32) * knew_ref[0], axis=-1, keepdims=True)
        m_o = m_ref[...]
        m_f = jnp.maximum(m_o, s_own)
        a = jnp.exp(m_o - m_f)
        p_own = jnp.exp(s_own - m_f)
        l = a * l_ref[...] + p_own
        vnew = vnew_ref[0]
        vrows = jnp.concatenate([jnp.broadcast_to(vnew[:, g * 128:(g + 1) * 128], (8, 128))
                                 for g in range(DIFF_KV_HEADS)], axis=0)
        o = (a * acc_ref[...] + p_own * vrows) / l
        lam_full = _lambda_full(lam_ref, lam_init)
        res = []
        for g in range(DIFF_KV_HEADS):
            o1 = o[g * 8:g * 8 + 4]
            o2 = o[g * 8 + 4:g * 8 + 8]
            res.append(_rms(o1 - lam_full * o2, subg_ref[...]) * (1.0 - lam_init))
        o_ref[0] = jnp.concatenate(res, axis=0).astype(BF16)


def _diff_decode(pt_flat, q_bd, knew, vnew, lam, subg, k_pages, v_pages, layer, n_pool, n_seq, n_pages, lam_init):
    halves = 2
    pps = n_pages // halves
    chunk = pps * LANE
    kern = functools.partial(_diff_decode_kernel, n_pages=n_pages, pages_per_step=pps, layer_off=layer * n_pool,
                             lam_init=lam_init)
    return pl.pallas_call(
        kern,
        grid_spec=pltpu.PrefetchScalarGridSpec(
            num_scalar_prefetch=1,
            grid=(n_seq * halves,),
            in_specs=[pl.BlockSpec((1, 16, 256), lambda s, pt: (s // halves, 0, 0)),
                      pl.BlockSpec((1, 1, 256), lambda s, pt: (s // halves, 0, 0)),
                      pl.BlockSpec((1, 1, 256), lambda s, pt: (s // halves, 0, 0)),
                      pl.BlockSpec((4, 64), lambda s, pt: (0, 0)),
                      pl.BlockSpec((1, 128), lambda s, pt: (0, 0)),
                      pl.BlockSpec(memory_space=pl.ANY), pl.BlockSpec(memory_space=pl.ANY)],
            out_specs=pl.BlockSpec((1, 8, 128), lambda s, pt: (s // halves, 0, 0)),
            scratch_shapes=[pltpu.VMEM((2, 256, chunk), F32), pltpu.VMEM((2, 2 * chunk, 128), F32),
                            pltpu.SemaphoreType.DMA((2,)),
                            pltpu.VMEM((16, 1), F32), pltpu.VMEM((16, 1), F32), pltpu.VMEM((16, 128), F32)]),
        out_shape=jax.ShapeDtypeStruct((n_seq, 8, 128), BF16),
        compiler_params=_cparams(("arbitrary",)),
        name="diff_decode",
    )(pt_flat, q_bd, knew, vnew, lam, subg, k_pages, v_pages)


def _rope_tables(pos):
    half = MLA_ROPE_DIM // 2
    inv = ROPE_BASE ** (-jnp.arange(half, dtype=F32) / half)
    ang = pos.astype(F32)[:, None] * inv[None, :]
    pad = jnp.zeros((pos.shape[0], LANE - MLA_ROPE_DIM), F32)
    cos, sin = jnp.cos(ang), jnp.sin(ang)
    return jnp.concatenate([cos, cos, pad], axis=1), jnp.concatenate([sin, sin, pad], axis=1)


def _swap_halves(w):
    half = w.shape[-1] // 2
    return jnp.concatenate([-w[..., half:], w[..., :half]], axis=-1)


def _pad_cols(w, width):
    return jnp.pad(w, [(0, 0)] * (w.ndim - 1) + [(0, width - w.shape[-1])])


def _pad_blocks(w, block, width):
    rows = w.shape[0]
    return _pad_cols(w.reshape(rows, -1, block), width).reshape(rows, -1)


def _ab_weights(w_in, w_uq, w_uk, w_uv):
    wq, wk, wv = w_in[:, 0:512], w_in[:, 512:640], w_in[:, 640:768]
    kpe_w = w_in[:, 1152:1184]
    w1 = jnp.concatenate([_pad_blocks(wq, 64, LANE), wk, wv, _pad_blocks(wk, 64, LANE), _pad_blocks(wv, 64, LANE),
                          w_in[:, 768:1152], _pad_cols(kpe_w, LANE), _pad_cols(_swap_halves(kpe_w), LANE)], axis=1)
    uq = w_uq.reshape(MLA_Q_RANK, MLA_HEADS, MLA_NOPE_DIM + MLA_ROPE_DIM)
    nope = uq[:, :, :MLA_NOPE_DIM].reshape(MLA_Q_RANK, MLA_HEADS * MLA_NOPE_DIM)
    pe = uq[:, :, MLA_NOPE_DIM:]
    pe_pad = _pad_cols(pe, LANE).reshape(MLA_Q_RANK, MLA_HEADS * LANE)
    pes_pad = _pad_cols(_swap_halves(pe), LANE).reshape(MLA_Q_RANK, MLA_HEADS * LANE)
    wuq = jnp.concatenate([nope, pe_pad, pes_pad], axis=1)
    wukt = jnp.transpose(w_uk, (1, 2, 0))
    eye = jnp.eye(MLA_HEADS, dtype=F32)
    wuv_bd = (jnp.transpose(w_uv, (1, 0, 2))[:, :, None, :] * eye[:, None, :, None]).reshape(
        MLA_HEADS * MLA_KV_RANK, MLA_HEADS * MLA_V_DIM)
    return w1.astype(BF16), wuq.astype(BF16), wukt.astype(BF16), wuv_bd.astype(BF16)


def _c_weights(w_in):
    wq, wk, wv = w_in[:, 0:1024], w_in[:, 1024:1280], w_in[:, 1280:1536]
    w = jnp.concatenate([_pad_blocks(wq, 64, LANE), wk, wv, _pad_blocks(wk, 64, LANE), _pad_blocks(wv, 128, 256)],
                        axis=1)
    return w.astype(BF16)


def _lane_table(rows, width, entries):
    lane = jnp.arange(width, dtype=I32)[None, :]
    tab = jnp.zeros((rows, width), F32)
    for ln, val in entries.items():
        tab = jnp.where(lane == ln, jnp.broadcast_to(jnp.asarray(val, F32).reshape(-1, 1), (rows, 1)), tab)
    return tab


def _query_consts(slopes, blocks_per_head):
    ent = {}
    for h, s in enumerate(slopes):
        for c in range(blocks_per_head):
            base = (h * blocks_per_head + c) * LANE
            ent[base + POS_HI_LANE] = s * 256.0
            ent[base + POS_LO_LANE] = s
    return _lane_table(1, len(slopes) * blocks_per_head * LANE, ent)


def _key_tables(seq):
    pos = jnp.arange(seq, dtype=I32)
    hi = (pos // 256).astype(F32)
    lo = (pos % 256).astype(F32)
    nb = seq // MOBA_BLOCK
    ent = {}
    for g in range(MOBA_KV_HEADS):
        ent[g * LANE + POS_HI_LANE] = hi
        ent[g * LANE + POS_LO_LANE] = lo
        for n in range(nb):
            ent[g * LANE + SEL_LANE0 + n] = (pos // MOBA_BLOCK == n).astype(F32)
        ent[(MOBA_KV_HEADS + g) * LANE + ONES_LANE] = jnp.ones((seq,), F32)
    ktab_ab = _lane_table(seq, 2 * MOBA_KV_HEADS * LANE, ent)
    ent = {}
    for gc in range(2 * DIFF_KV_HEADS):
        ent[gc * LANE + POS_HI_LANE] = hi
        ent[gc * LANE + POS_LO_LANE] = lo
    for g in range(DIFF_KV_HEADS):
        ent[512 + g * 256 + 128] = jnp.ones((seq,), F32)
    ktab_c = _lane_table(seq, 1024, ent)
    return ktab_ab, ktab_c


def _block_diag_rows(q, blocks, width):
    n, rows, _ = q.shape
    nblk = max(blocks) + 1
    sel = jnp.asarray([[1.0 if blocks[r] == b else 0.0 for b in range(nblk)] for r in range(rows)], q.dtype)
    return (q[:, :, None, :] * sel[None, :, :, None]).reshape(n, rows, nblk * width)


def kernel(x_prompt, x_sample, c_prompt, c_sample, cache_moba_k, cache_moba_v, cache_mla_ckv, cache_mla_kpe,
           cache_diff_k, cache_diff_v, page_table, mod_w, mod_b, norm_mix, norm_ffn, ab_w_in, mla_q_norm,
           mla_kv_norm, mla_w_uq, mla_w_uk, mla_w_uv, ab_w_out, c_w_in, diff_lambda, diff_subln, c_w_out,
           moe_w_group, moe_b_group, moe_w_expert, moe_b_expert, moe_w1, moe_w3, moe_w2, final_norm):
    batch, seq, d = x_prompt.shape
    n_seq = x_sample.shape[0]
    depth = mod_w.shape[0]
    n_pool = cache_moba_k.shape[1]
    n_pages = page_table.shape[1]
    past_len = n_pages * cache_moba_k.shape[2]
    nb = seq // MOBA_BLOCK
    assert d == D_MODEL and x_sample.shape[1] == 1 and cache_moba_k.shape[2] == LANE
    assert seq % MOBA_BLOCK == 0 and past_len % MOBA_BLOCK == 0 and n_seq % 8 == 0
    assert nb <= SEL_LANES and seq <= 256 * 256
    n_p = batch * seq
    tpb = seq // TOKEN_TILE

    xp = x_prompt.reshape(n_p, d)
    xs = x_sample.reshape(n_seq, d)
    mod = _modulation(jnp.concatenate([c_prompt, c_sample], axis=0), mod_w, mod_b)
    mod = mod.reshape(depth, batch + n_seq, 6, d)
    pt_flat = page_table.reshape(-1)

    n_all = cache_moba_k.shape[0] * n_pool
    mk_pages = jnp.transpose(cache_moba_k, (0, 1, 3, 4, 2)).reshape(n_all, 128, LANE)
    mv_pages = jnp.transpose(cache_moba_v, (0, 1, 3, 4, 2)).reshape(n_all, 128, LANE)
    ckv_pages = cache_mla_ckv.reshape(n_all, LANE, 128)
    kpe_pages = jnp.transpose(cache_mla_kpe, (0, 1, 3, 2)).reshape(n_all, MLA_ROPE_DIM, LANE)
    n_all_c = cache_diff_k.shape[0] * n_pool
    dk_pages = jnp.transpose(cache_diff_k, (0, 1, 3, 4, 5, 2)).reshape(n_all_c, 256, LANE)
    dv_pages = cache_diff_v.reshape(n_all_c, 2 * LANE, 128)

    cos_p, sin_p = _rope_tables(jnp.arange(seq, dtype=I32))
    cos_s, sin_s = _rope_tables(jnp.full((1,), past_len, I32))
    ktab_ab, ktab_c = _key_tables(seq)
    ktab_ab_s = jnp.zeros((1, ktab_ab.shape[1]), F32)
    ktab_c_s = jnp.zeros((1, ktab_c.shape[1]), F32)
    qconst_ab = _query_consts(_alibi_slopes(MOBA_HEADS), 1)
    qconst_c = _query_consts(_alibi_slopes(DIFF_HEADS), 2)
    kones = _lane_table(1, LANE, {MLA_ONES_LANE - MLA_KV_RANK: 1.0})

    n_exp_all = depth * N_EXPERTS
    w1_all = moe_w1.reshape(n_exp_all, d, EXPERT_FF)
    w3_all = moe_w3.reshape(n_exp_all, d, EXPERT_FF)
    w2_all = moe_w2.reshape(n_exp_all, EXPERT_FF, d)
    n_slots = TOP_EXPERTS * (n_p + n_seq)
    n_tiles_max = -(-n_slots // EXPERT_TILE) + N_EXPERTS
    x_sorted = jnp.zeros((n_tiles_max * EXPERT_TILE, 512), U32)
    fg = final_norm.reshape(1, d)

    moba_blocks = [h // (MOBA_HEADS // MOBA_KV_HEADS) for h in range(MOBA_HEADS)]
    diff_rows = [(g, c, j) for g in range(DIFF_KV_HEADS) for c in range(2) for j in range(4)]

    new_p = {k: [] for k in ("mk", "mv", "ckv", "kpe", "dk", "dv")}
    new_s = {k: [] for k in ("mk", "mv", "ckv", "kpe", "dk", "dv")}
    for l in range(depth):
        i = l // 2
        mp = [mod[l, :batch, k].reshape(batch, 1, d) for k in range(6)]
        ms = [mod[l, batch:, k] for k in range(6)]
        g_mix = norm_mix[l].reshape(1, d)
        if l % 2 == 0:
            w1, wuq, wukt, wuv_bd = _ab_weights(ab_w_in[i], mla_w_uq[i], mla_w_uk[i], mla_w_uv[i])
            wout = ab_w_out[i].astype(BF16)
            qg = mla_q_norm[i].reshape(1, -1)
            kvg = mla_kv_norm[i].reshape(1, -1)
            qt, mk, mv, ka, vt, kmean, qft, ckv, kpe, kfull, kft = _inproj_ab(
                xp, mp[0], mp[1], g_mix, w1, qg, kvg, wuq, wukt, cos_p, sin_p, qconst_ab, ktab_ab, kones,
                False, TOKEN_TILE, tpb)
            km = kmean.reshape(batch, tpb, MOBA_KV_HEADS, 64).transpose(0, 2, 1, 3)
            km = jnp.pad(km, ((0, 0), (0, 0), (0, SEL_LANES - tpb), (0, LANE - 64)))
            oa = _moba_prompt(qt, ka, vt, km, batch, seq)
            olat = _mla_prompt(qft, kfull, kft, batch, seq)
            xp = _outproj_ab(xp, mp[2], oa, olat, wuv_bd, wout, False, TOKEN_TILE, tpb)
            new_p["mk"].append(mk); new_p["mv"].append(mv); new_p["ckv"].append(ckv); new_p["kpe"].append(kpe)
            qt, mk, mv, ka, vt, kmean, qft, ckv, kpe, kfull, kft = _inproj_ab(
                xs, ms[0], ms[1], g_mix, w1, qg, kvg, wuq, wukt, cos_s, sin_s, qconst_ab, ktab_ab_s, kones,
                True, n_seq, 1)
            q_heads = jnp.transpose(qt[0].reshape(MOBA_HEADS, LANE, n_seq), (2, 0, 1))
            q_bd = _block_diag_rows(q_heads[:, :, :64], moba_blocks, 64)
            o_raw = _moba_decode(pt_flat, q_bd, mk.reshape(n_seq, 1, 128), mv.reshape(n_seq, 1, 128),
                                 mk_pages, mv_pages, i, n_pool, n_seq, n_pages)
            o_raw = o_raw.reshape(n_seq, MOBA_KV_HEADS, 4, MOBA_KV_HEADS, 64)
            oa = jnp.concatenate([o_raw[:, g, :, g, :] for g in range(MOBA_KV_HEADS)], axis=1)
            oa = oa.reshape(n_seq, 512).astype(BF16)
            qf3 = jnp.transpose(qft[0].reshape(MLA_HEADS, 256, n_seq), (2, 0, 1))
            olat = _mla_decode(pt_flat, qf3[:, :, :128], qf3[:, :, 128:128 + MLA_ROPE_DIM],
                               ckv.reshape(n_seq, 1, 128), kpe.reshape(n_seq, 1, MLA_ROPE_DIM),
                               ckv_pages, kpe_pages, i, n_pool, n_seq, n_pages)
            xs = _outproj_ab(xs, ms[2], oa, olat.reshape(n_seq, 1024), wuv_bd, wout, True, n_seq, 1)
            new_s["mk"].append(mk); new_s["mv"].append(mv); new_s["ckv"].append(ckv); new_s["kpe"].append(kpe)
        else:
            lam_init = 0.8 - 0.6 * math.exp(-0.3 * l)
            wc = _c_weights(c_w_in[i])
            wout = c_w_out[i].astype(BF16)
            lam = diff_lambda[i]
            subg = diff_subln[i].reshape(1, -1)
            qt, k, v, ka, vt = _inproj_c(xp, mp[0], mp[1], g_mix, wc, qconst_c, ktab_c, False, TOKEN_TILE, tpb)
            o = _diff_prompt(qt, ka, vt, lam, subg, lam_init, batch, seq)
            xp = _outproj_c(xp, mp[2], o, wout, False, TOKEN_TILE, tpb)
            new_p["dk"].append(k); new_p["dv"].append(v)
            qt, k, v, ka, vt = _inproj_c(xs, ms[0], ms[1], g_mix, wc, qconst_c, ktab_c_s, True, n_seq, 1)
            q4 = jnp.transpose(qt[0].reshape(DIFF_HEADS, 2, LANE, n_seq), (3, 0, 1, 2))[:, :, :, :64]
            q_rows = jnp.stack([q4[:, g * 4 + j, c, :] for (g, c, j) in diff_rows], axis=1)
            q_bd = _block_diag_rows(q_rows, [g * 2 + c for (g, c, j) in diff_rows], 64)
            o = _diff_decode(pt_flat, q_bd, k.reshape(n_seq, 1, 256), v.reshape(n_seq, 1, 256), lam, subg,
                             dk_pages, dv_pages, i, n_pool, n_seq, n_pages, lam_init)
            xs = _outproj_c(xs, ms[2], o.reshape(n_seq, 1024), wout, True, n_seq, 1)
            new_s["dk"].append(k); new_s["dv"].append(v)
        wr = jnp.concatenate([moe_w_expert[l], moe_w_group[l],
                              jnp.zeros((d, LANE - N_EXPERTS - N_GROUPS), F32)], axis=1)
        br = jnp.concatenate([moe_b_expert[l], moe_b_group[l],
                              jnp.zeros((LANE - N_EXPERTS - N_GROUPS,), F32)]).reshape(1, LANE)
        wr_hi = wr.astype(BF16)
        wr2 = jnp.stack([wr_hi, (wr - wr_hi.astype(F32)).astype(BF16)])
        g_ffn = norm_ffn[l].reshape(1, d)
        hp_p, info_p = _router(xp, mp[3], mp[4], g_ffn, wr2, br, False, TOKEN_TILE, tpb)
        hp_s, info_s = _router(xs, ms[3], ms[4], g_ffn, wr2, br, True, n_seq, 1)
        ids = jnp.concatenate([info_p[:, 0:TOP_EXPERTS], info_s[:, 0:TOP_EXPERTS]], axis=0).astype(I32).reshape(-1)
        dest, tile_expert, n_tiles = _routing_offsets(ids, n_tiles_max)
        dest_p = dest[:TOP_EXPERTS * n_p].reshape(n_p // TOKEN_TILE, 1, TOP_EXPERTS * TOKEN_TILE)
        dest_s = dest[TOP_EXPERTS * n_p:].reshape(1, 1, TOP_EXPERTS * n_seq)
        x_sorted = _dispatch(dest_p, hp_p, x_sorted, TOKEN_TILE)
        x_sorted = _dispatch(dest_s, hp_s, x_sorted, n_seq)
        y_sorted = _experts(tile_expert + l * N_EXPERTS, n_tiles, x_sorted, w1_all, w3_all, w2_all)
        last = l == depth - 1
        xp = _combine(dest_p, info_p, xp, mp[5], fg, y_sorted, False, TOKEN_TILE, tpb, last)
        xs = _combine(dest_s, info_s, xs, ms[5], fg, y_sorted, True, n_seq, 1, last)

    y_prompt = xp.reshape(batch, seq, d)
    y_sample = xs.reshape(n_seq, 1, d)

    def stack(lst, shape):
        return jnp.stack([a.reshape(shape) for a in lst])

    outs_p = (stack(new_p["mk"], (batch, seq, MOBA_KV_HEADS, MOBA_HEAD_DIM)),
              stack(new_p["mv"], (batch, seq, MOBA_KV_HEADS, MOBA_HEAD_DIM)),
              stack(new_p["ckv"], (batch, seq, MLA_KV_RANK)),
              stack(new_p["kpe"], (batch, seq, MLA_ROPE_DIM)),
              stack(new_p["dk"], (batch, seq, DIFF_KV_HEADS, 2, DIFF_HEAD_DIM)),
              stack(new_p["dv"], (batch, seq, DIFF_KV_HEADS, 2 * DIFF_HEAD_DIM)))
    outs_s = (stack(new_s["mk"], (n_seq, 1, MOBA_KV_HEADS, MOBA_HEAD_DIM)),
              stack(new_s["mv"], (n_seq, 1, MOBA_KV_HEADS, MOBA_HEAD_DIM)),
              stack(new_s["ckv"], (n_seq, 1, MLA_KV_RANK)),
              stack(new_s["kpe"], (n_seq, 1, MLA_ROPE_DIM)),
              stack(new_s["dk"], (n_seq, 1, DIFF_KV_HEADS, 2, DIFF_HEAD_DIM)),
              stack(new_s["dv"], (n_seq, 1, DIFF_KV_HEADS, 2 * DIFF_HEAD_DIM)))
    return (y_prompt, y_sample) + outs_p + outs_s
```

```python
import functools
import math

import jax
import jax.numpy as jnp
from jax import lax
from jax.experimental import pallas as pl
from jax.experimental.pallas import tpu as pltpu

F32 = jnp.float32
BF16 = jnp.bfloat16
U32 = jnp.uint32
I32 = jnp.int32
HIGHEST = lax.Precision.HIGHEST

D_MODEL = 1024
MOBA_HEADS, MOBA_KV_HEADS, MOBA_HEAD_DIM, MOBA_BLOCK, MOBA_TOPK = 8, 2, 64, 256, 3
MLA_HEADS, MLA_Q_RANK, MLA_KV_RANK, MLA_NOPE_DIM, MLA_ROPE_DIM, MLA_V_DIM = 8, 256, 128, 64, 32, 64
ROPE_BASE = 10000.0
DIFF_HEADS, DIFF_KV_HEADS, DIFF_HEAD_DIM = 8, 2, 64
N_GROUPS, EXPERTS_PER_GROUP, EXPERT_FF, TOP_EXPERTS = 4, 8, 256, 2
N_EXPERTS = N_GROUPS * EXPERTS_PER_GROUP
EPS = 1e-6
NEG = -1e30

LANE = 128
VMEM_LIMIT = 56 * 1024 * 1024
TOKEN_TILE = 256
WIDE_TILE = 512
EXPERT_TILE = 256
SEL_LANE0, SEL_LANES = 64, 16
POS_HI_LANE, POS_LO_LANE = 80, 81
ONES_LANE = 64
MOBA_V_ROWS = 80
DIFF_V_ROWS = 144
MLA_V_ROWS = 176
MLA_ONES_LANE = MLA_KV_RANK + MLA_ROPE_DIM
AB_IN_EXT = 1024 + 256 + 512 + 256 + 128 + 128 + 128
UQ_EXT = 512 + 1024 + 1024
C_IN_EXT = 2048 + 256 + 256 + 512 + 512


def _cparams(sem):
    return pltpu.CompilerParams(dimension_semantics=sem, vmem_limit_bytes=VMEM_LIMIT)


def _alibi_slopes(n):
    return [2.0 ** (-8.0 * (i + 1) / n) for i in range(n)]


def _slope_column(values):
    row = lax.broadcasted_iota(I32, (len(values), 1), 0)
    col = jnp.zeros((len(values), 1), F32)
    for r, v in enumerate(values):
        col = jnp.where(row == r, v, col)
    return col


def _rms(x, g):
    return x * lax.rsqrt(jnp.mean(x * x, axis=-1, keepdims=True) + EPS) * g


def _ada(x, g, shift, scale):
    return _rms(x, g) * (1.0 + scale) + shift


def _mod_kernel(c_ref, w_ref, b_ref, o_ref):
    c = c_ref[...]
    a = c / (1.0 + jnp.exp(-c))
    o_ref[0] = jnp.dot(a, w_ref[0], preferred_element_type=F32, precision=HIGHEST) + b_ref[0]


def _modulation(c_all, mod_w, mod_b):
    depth, d, n6 = mod_w.shape
    rows = c_all.shape[0]
    tn = 1024
    return pl.pallas_call(
        _mod_kernel,
        grid=(depth, n6 // tn),
        in_specs=[
            pl.BlockSpec((rows, d), lambda l, j: (0, 0)),
            pl.BlockSpec((1, d, tn), lambda l, j: (l, 0, j)),
            pl.BlockSpec((1, 1, tn), lambda l, j: (l, 0, j)),
        ],
        out_specs=pl.BlockSpec((1, rows, tn), lambda l, j: (l, 0, j)),
        out_shape=jax.ShapeDtypeStruct((depth, rows, n6), F32),
        compiler_params=_cparams(("arbitrary", "arbitrary")),
        name="modulation",
    )(c_all, mod_w, mod_b.reshape(depth, 1, n6))


def _mod_spec(per_row, tm, tiles_per_batch):
    if per_row:
        return pl.BlockSpec((tm, D_MODEL), lambda t: (t, 0))
    return pl.BlockSpec((None, 1, D_MODEL), lambda t: (t // tiles_per_batch, 0, 0))


def _pos_spec(per_row, tm, width, tiles_per_batch):
    if per_row:
        return pl.BlockSpec((1, width), lambda t: (0, 0))
    return pl.BlockSpec((tm, width), lambda t: (t % tiles_per_batch, 0))


def _full_spec(shape):
    nd = len(shape)
    return pl.BlockSpec(shape, lambda t: (0,) * nd)


def _inproj_ab_kernel(x_ref, sh_ref, sc_ref, g_ref, w1_ref, qg_ref, kvg_ref, wuq_ref, wukt_ref, cos_ref, sin_ref,
                      qconst_ref, ktab_ref, kones_ref,
                      qt_ref, mk_ref, mv_ref, ka_ref, vt_ref, kmean_ref, qft_ref, ckv_ref, kpe_ref, kfull_ref, kft_ref):
    h = _ada(x_ref[...], g_ref[...], sh_ref[...], sc_ref[...]).astype(BF16)
    z = jnp.dot(h, w1_ref[...], preferred_element_type=F32)
    qt_ref[0] = (z[:, 0:1024] * (MOBA_HEAD_DIM ** -0.5) + qconst_ref[...]).T.astype(BF16)
    mk = z[:, 1024:1152]
    mk_ref[...] = mk
    mv_ref[...] = z[:, 1152:1280]
    ka_ref[...] = (z[:, 1280:1536] + ktab_ref[:, 0:256]).astype(BF16)
    vt_ref[0] = (z[:, 1536:1792] + ktab_ref[:, 256:512]).T.astype(BF16)
    kmean_ref[0] = jnp.mean(mk, axis=0, keepdims=True)
    cos = cos_ref[...]
    sin = sin_ref[...]
    ckv = _rms(z[:, 2048:2176], kvg_ref[...])
    ckv_ref[...] = ckv
    kpe = z[:, 2176:2304] * cos + z[:, 2304:2432] * sin
    kpe_ref[...] = kpe[:, 0:MLA_ROPE_DIM]
    kpe1 = kpe + kones_ref[...]
    kfull_ref[:, 0:128] = ckv.astype(BF16)
    kfull_ref[:, 128:256] = kpe1.astype(BF16)
    kft_ref[0, 0:128, :] = ckv.T.astype(BF16)
    kft_ref[0, 128:256, :] = kpe1.T.astype(BF16)
    cqn = _rms(z[:, 1792:2048], qg_ref[...]).astype(BF16)
    q2 = jnp.dot(cqn, wuq_ref[...], preferred_element_type=F32)
    scale = (MLA_NOPE_DIM + MLA_ROPE_DIM) ** -0.5
    for hd in range(MLA_HEADS):
        nope = q2[:, hd * 64:(hd + 1) * 64].astype(BF16)
        qlat = jnp.dot(nope, wukt_ref[hd], preferred_element_type=F32)
        qft_ref[0, hd * 256:hd * 256 + 128, :] = (qlat * scale).T.astype(BF16)
        pe = q2[:, 512 + hd * 128:512 + (hd + 1) * 128] * cos + q2[:, 1536 + hd * 128:1536 + (hd + 1) * 128] * sin
        qft_ref[0, hd * 256 + 128:(hd + 1) * 256, :] = (pe * scale).T.astype(BF16)


def _inproj_ab(x, shift, scale, g, w1, qg, kvg, wuq, wukt, cos_t, sin_t, qconst, ktab, kones,
               per_row, tm, tiles_per_batch):
    n = x.shape[0]
    nt = n // tm
    row = lambda w: pl.BlockSpec((tm, w), lambda t: (t, 0))
    pos = lambda w: _pos_spec(per_row, tm, w, tiles_per_batch)
    tposed = lambda feats: (jax.ShapeDtypeStruct((nt, feats, tm), BF16),
                            pl.BlockSpec((1, feats, tm), lambda t: (t, 0, 0)))
    outs = [
        tposed(1024),
        (jax.ShapeDtypeStruct((n, 128), F32), row(128)),
        (jax.ShapeDtypeStruct((n, 128), F32), row(128)),
        (jax.ShapeDtypeStruct((n, 256), BF16), row(256)),
        tposed(256),
        (jax.ShapeDtypeStruct((nt, 1, 128), F32), pl.BlockSpec((1, 1, 128), lambda t: (t, 0, 0))),
        tposed(MLA_HEADS * 256),
        (jax.ShapeDtypeStruct((n, 128), F32), row(128)),
        (jax.ShapeDtypeStruct((n, MLA_ROPE_DIM), F32), row(MLA_ROPE_DIM)),
        (jax.ShapeDtypeStruct((n, 256), BF16), row(256)),
        tposed(256),
    ]
    return pl.pallas_call(
        _inproj_ab_kernel,
        grid=(nt,),
        in_specs=[
            row(D_MODEL), _mod_spec(per_row, tm, tiles_per_batch), _mod_spec(per_row, tm, tiles_per_batch),
            _full_spec((1, D_MODEL)), _full_spec((D_MODEL, AB_IN_EXT)), _full_spec((1, MLA_Q_RANK)),
            _full_spec((1, MLA_KV_RANK)), _full_spec((MLA_Q_RANK, UQ_EXT)), _full_spec((MLA_HEADS, 64, 128)),
            pos(LANE), pos(LANE), _full_spec((1, 1024)), pos(512), _full_spec((1, LANE)),
        ],
        out_specs=[o[1] for o in outs],
        out_shape=[o[0] for o in outs],
        compiler_params=_cparams(("arbitrary",)),
        name="inproj_ab",
    )(x, shift, scale, g, w1, qg, kvg, wuq, wukt, cos_t, sin_t, qconst, ktab, kones)


def _inproj_c_kernel(x_ref, sh_ref, sc_ref, g_ref, w_ref, qconst_ref, ktab_ref, qt_ref, k_ref, v_ref, ka_ref, vt_ref):
    h = _ada(x_ref[...], g_ref[...], sh_ref[...], sc_ref[...]).astype(BF16)
    z = jnp.dot(h, w_ref[...], preferred_element_type=F32)
    qt_ref[0] = (z[:, 0:2048] * (DIFF_HEAD_DIM ** -0.5) + qconst_ref[...]).T.astype(BF16)
    k_ref[...] = z[:, 2048:2304]
    v_ref[...] = z[:, 2304:2560]
    ka_ref[...] = (z[:, 2560:3072] + ktab_ref[:, 0:512]).astype(BF16)
    vt_ref[0] = (z[:, 3072:3584] + ktab_ref[:, 512:1024]).T.astype(BF16)


def _inproj_c(x, shift, scale, g, w, qconst, ktab, per_row, tm, tiles_per_batch):
    n = x.shape[0]
    nt = n // tm
    row = lambda wd: pl.BlockSpec((tm, wd), lambda t: (t, 0))
    tspec = lambda feats: pl.BlockSpec((1, feats, tm), lambda t: (t, 0, 0))
    return pl.pallas_call(
        _inproj_c_kernel,
        grid=(nt,),
        in_specs=[row(D_MODEL), _mod_spec(per_row, tm, tiles_per_batch), _mod_spec(per_row, tm, tiles_per_batch),
                  _full_spec((1, D_MODEL)), _full_spec((D_MODEL, C_IN_EXT)), _full_spec((1, 2048)),
                  _pos_spec(per_row, tm, 1024, tiles_per_batch)],
        out_specs=[tspec(2048), row(256), row(256), row(512), tspec(512)],
        out_shape=[jax.ShapeDtypeStruct((nt, 2048, tm), BF16), jax.ShapeDtypeStruct((n, 256), F32),
                   jax.ShapeDtypeStruct((n, 256), F32), jax.ShapeDtypeStruct((n, 512), BF16),
                   jax.ShapeDtypeStruct((nt, 512, tm), BF16)],
        compiler_params=_cparams(("arbitrary",)),
        name="inproj_c",
    )(x, shift, scale, g, w, qconst, ktab)


def _outproj_ab_kernel(x_ref, gate_ref, oa_ref, olat_ref, wuv_ref, wout_ref, o_ref):
    ob = jnp.dot(olat_ref[...], wuv_ref[...], preferred_element_type=F32).astype(BF16)
    y = jnp.dot(oa_ref[...], wout_ref[0:512, :], preferred_element_type=F32)
    y = y + jnp.dot(ob, wout_ref[512:1024, :], preferred_element_type=F32)
    o_ref[...] = x_ref[...] + gate_ref[...] * y


def _outproj_ab(x, gate, oa, olat, wuv_bd, wout, per_row, tm, tiles_per_batch):
    n = x.shape[0]
    row = lambda wd: pl.BlockSpec((tm, wd), lambda t: (t, 0))
    return pl.pallas_call(
        _outproj_ab_kernel,
        grid=(n // tm,),
        in_specs=[row(D_MODEL), _mod_spec(per_row, tm, tiles_per_batch), row(512), row(1024),
                  _full_spec((1024, 512)), _full_spec((1024, D_MODEL))],
        out_specs=row(D_MODEL),
        out_shape=jax.ShapeDtypeStruct((n, D_MODEL), F32),
        compiler_params=_cparams(("arbitrary",)),
        name="outproj_ab",
    )(x, gate, oa, olat, wuv_bd, wout)


def _outproj_c_kernel(x_ref, gate_ref, o_in_ref, wout_ref, o_ref):
    y = jnp.dot(o_in_ref[...], wout_ref[...], preferred_element_type=F32)
    o_ref[...] = x_ref[...] + gate_ref[...] * y


def _outproj_c(x, gate, o_in, wout, per_row, tm, tiles_per_batch):
    n = x.shape[0]
    row = lambda wd: pl.BlockSpec((tm, wd), lambda t: (t, 0))
    return pl.pallas_call(
        _outproj_c_kernel,
        grid=(n // tm,),
        in_specs=[row(D_MODEL), _mod_spec(per_row, tm, tiles_per_batch), row(1024), _full_spec((1024, D_MODEL))],
        out_specs=row(D_MODEL),
        out_shape=jax.ShapeDtypeStruct((n, D_MODEL), F32),
        compiler_params=_cparams(("arbitrary",)),
        name="outproj_c",
    )(x, gate, o_in, wout)


def _router_kernel(x_ref, sh_ref, sc_ref, g_ref, wr_ref, br_ref, hp_ref, info_ref):
    h = _ada(x_ref[...], g_ref[...], sh_ref[...], sc_ref[...])
    h_hi = h.astype(BF16)
    hp_ref[...] = h_hi.astype(F32)
    h_lo = (h - h_hi.astype(F32)).astype(BF16)
    logit = (jnp.dot(h_hi, wr_ref[0], preferred_element_type=F32)
             + jnp.dot(h_lo, wr_ref[0], preferred_element_type=F32)
             + jnp.dot(h_hi, wr_ref[1], preferred_element_type=F32)) + br_ref[...]
    lane = lax.broadcasted_iota(I32, logit.shape, 1)
    is_grp = (lane >= N_EXPERTS) & (lane < N_EXPERTS + N_GROUPS)
    lg = jnp.where(is_grp, logit, -jnp.inf)
    mg = jnp.max(lg, axis=-1, keepdims=True)
    g_w = 1.0 / jnp.sum(jnp.exp(lg - mg), axis=-1, keepdims=True)
    g_idx = jnp.min(jnp.where(lg == mg, lane, 4 * LANE), axis=-1, keepdims=True) - N_EXPERTS
    in_grp = (lane < N_EXPERTS) & ((lane // EXPERTS_PER_GROUP) == g_idx)
    le = jnp.where(in_grp, logit, -jnp.inf)
    m1 = jnp.max(le, axis=-1, keepdims=True)
    i1 = jnp.min(jnp.where(le == m1, lane, 4 * LANE), axis=-1, keepdims=True)
    le2 = jnp.where(lane == i1, -jnp.inf, le)
    m2 = jnp.max(le2, axis=-1, keepdims=True)
    i2 = jnp.min(jnp.where(le2 == m2, lane, 4 * LANE), axis=-1, keepdims=True)
    r = jnp.exp(m2 - m1)
    w1 = g_w / (1.0 + r)
    w2 = g_w * r / (1.0 + r)
    info = jnp.where(lane == 0, i1.astype(F32), 0.0) + jnp.where(lane == 1, i2.astype(F32), 0.0)
    info_ref[...] = info + jnp.where(lane == 2, w1, 0.0) + jnp.where(lane == 3, w2, 0.0)


def _router(x, shift, scale, g, wr, br, per_row, tm, tiles_per_batch):
    n = x.shape[0]
    row = lambda wd: pl.BlockSpec((tm, wd), lambda t: (t, 0))
    return pl.pallas_call(
        _router_kernel,
        grid=(n // tm,),
        in_specs=[row(D_MODEL), _mod_spec(per_row, tm, tiles_per_batch), _mod_spec(per_row, tm, tiles_per_batch),
                  _full_spec((1, D_MODEL)), _full_spec((2, D_MODEL, LANE)), _full_spec((1, LANE))],
        out_specs=[row(D_MODEL), row(LANE)],
        out_shape=[jax.ShapeDtypeStruct((n, D_MODEL), F32), jax.ShapeDtypeStruct((n, LANE), F32)],
        compiler_params=_cparams(("arbitrary",)),
        name="router",
    )(x, shift, scale, g, wr, br)


def _routing_offsets(expert_ids, n_tiles_max):
    onehot = (expert_ids[:, None] == jnp.arange(N_EXPERTS, dtype=I32)[None, :]).astype(I32)
    csum = jnp.cumsum(onehot, axis=0)
    counts = csum[-1]
    rank = jnp.sum(onehot * csum, axis=1) - 1
    padded = ((counts + EXPERT_TILE - 1) // EXPERT_TILE) * EXPERT_TILE
    ends = jnp.cumsum(padded)
    starts = ends - padded
    dest = jnp.sum(onehot * starts[None, :], axis=1) + rank
    tile_rows = jnp.arange(n_tiles_max, dtype=I32) * EXPERT_TILE
    tile_expert = jnp.minimum(jnp.sum((tile_rows[:, None] >= ends[None, :]).astype(I32), axis=1), N_EXPERTS - 1)
    n_tiles = (ends[-1] // EXPERT_TILE).reshape(1)
    return dest.astype(I32), tile_expert.astype(I32), n_tiles.astype(I32)


def _dispatch_kernel(dest_ref, hp_ref, xs_in_ref, xs_ref, sem, *, tm):
    del xs_in_ref

    for r in range(tm):
        for k in range(TOP_EXPERTS):
            d = dest_ref[0, 0, TOP_EXPERTS * r + k]
            pltpu.make_async_copy(hp_ref.at[pl.ds(r, 1)], xs_ref.at[pl.ds(d, 1)], sem.at[0]).start(priority=k % 2)
    for k in range(TOP_EXPERTS):
        pltpu.make_async_copy(hp_ref, xs_ref.at[pl.ds(0, tm)], sem.at[0]).wait()


def _dispatch(dest_tiles, hp, xs, tm):
    n = hp.shape[0]
    return pl.pallas_call(
        functools.partial(_dispatch_kernel, tm=tm),
        grid=(n // tm,),
        in_specs=[pl.BlockSpec((1, 1, TOP_EXPERTS * tm), lambda t: (t, 0, 0), memory_space=pltpu.SMEM),
                  pl.BlockSpec((tm, D_MODEL), lambda t: (t, 0)),
                  pl.BlockSpec(memory_space=pl.ANY)],
        out_specs=pl.BlockSpec(memory_space=pl.ANY),
        out_shape=jax.ShapeDtypeStruct(xs.shape, F32),
        scratch_shapes=[pltpu.SemaphoreType.DMA((1,))],
        input_output_aliases={2: 0},
        compiler_params=_cparams(("arbitrary",)),
        name="moe_dispatch",
    )(dest_tiles, hp, xs)


def _experts_kernel(te_ref, nt_ref, xs_ref, w1_ref, w3_ref, w2_ref, y_ref):
    del te_ref

    @pl.when(pl.program_id(0) < nt_ref[0])
    def _():
        x = xs_ref[...].astype(BF16)
        a = jnp.dot(x, w1_ref[0].astype(BF16), preferred_element_type=F32)
        b = jnp.dot(x, w3_ref[0].astype(BF16), preferred_element_type=F32)
        hid = (a / (1.0 + jnp.exp(-a))) * b
        y_ref[...] = jnp.dot(hid.astype(BF16), w2_ref[0].astype(BF16), preferred_element_type=F32)

    @pl.when(pl.program_id(0) >= nt_ref[0])
    def _():
        y_ref[...] = jnp.zeros_like(y_ref)


def _experts(tile_expert, n_tiles, xs, w1, w3, w2):
    n_tiles_max = xs.shape[0] // EXPERT_TILE
    live = lambda t, te, nt: jnp.minimum(t, nt[0] - 1)
    wspec = lambda a, b: pl.BlockSpec((1, a, b), lambda t, te, nt: (te[live(t, te, nt)], 0, 0))
    return pl.pallas_call(
        _experts_kernel,
        grid_spec=pltpu.PrefetchScalarGridSpec(
            num_scalar_prefetch=2,
            grid=(n_tiles_max,),
            in_specs=[pl.BlockSpec((EXPERT_TILE, D_MODEL), lambda t, te, nt: (live(t, te, nt), 0)),
                      wspec(D_MODEL, EXPERT_FF), wspec(D_MODEL, EXPERT_FF), wspec(EXPERT_FF, D_MODEL)],
            out_specs=pl.BlockSpec((EXPERT_TILE, D_MODEL), lambda t, te, nt: (t, 0))),
        out_shape=jax.ShapeDtypeStruct((xs.shape[0], D_MODEL), F32),
        compiler_params=_cparams(("arbitrary",)),
        name="moe_experts",
    )(tile_expert, n_tiles, xs, w1, w3, w2)


def _combine_kernel(dest_ref, dest_next_ref, info_ref, x_ref, gmod_ref, fg_ref, y_hbm, o_ref, ybuf, sem, *,
                    tm, final):
    t = pl.program_id(0)
    nt = pl.num_programs(0)
    slot = t % 2

    def start_all(dref, slot_):
        for r in range(tm):
            for k in range(TOP_EXPERTS):
                d = dref[0, 0, TOP_EXPERTS * r + k]
                pltpu.make_async_copy(y_hbm.at[pl.ds(d, 1)], ybuf.at[slot_, pl.ds(k * tm + r, 1)],
                                      sem.at[slot_]).start(priority=k % 2)

    @pl.when(t == 0)
    def _():
        start_all(dest_ref, 0)

    @pl.when(t + 1 < nt)
    def _():
        start_all(dest_next_ref, 1 - slot)

    pltpu.make_async_copy(y_hbm.at[pl.ds(0, TOP_EXPERTS * tm)], ybuf.at[slot], sem.at[slot]).wait()
    info = info_ref[...]
    y = info[:, 2:3] * ybuf[slot, 0:tm] + info[:, 3:4] * ybuf[slot, tm:2 * tm]
    x_new = x_ref[...] + gmod_ref[...] * y
    o_ref[...] = _rms(x_new, fg_ref[...]) if final else x_new


def _combine(dest_tiles, info, x, gmod, fg, y_sorted, per_row, tm, tiles_per_batch, final):
    n = x.shape[0]
    nt = n // tm
    row = lambda wd: pl.BlockSpec((tm, wd), lambda t: (t, 0))
    dspec = lambda f: pl.BlockSpec((1, 1, TOP_EXPERTS * tm), lambda t: (f(t), 0, 0), memory_space=pltpu.SMEM)
    return pl.pallas_call(
        functools.partial(_combine_kernel, tm=tm, final=final),
        grid=(nt,),
        in_specs=[dspec(lambda t: t), dspec(lambda t: jnp.minimum(t + 1, nt - 1)), row(LANE), row(D_MODEL),
                  _mod_spec(per_row, tm, tiles_per_batch), _full_spec((1, D_MODEL)),
                  pl.BlockSpec(memory_space=pl.ANY)],
        out_specs=row(D_MODEL),
        out_shape=jax.ShapeDtypeStruct((n, D_MODEL), F32),
        scratch_shapes=[pltpu.VMEM((2, TOP_EXPERTS * tm, D_MODEL), F32), pltpu.SemaphoreType.DMA((2,))],
        compiler_params=_cparams(("arbitrary",)),
        name="moe_combine",
    )(dest_tiles, dest_tiles, info, x, gmod, fg, y_sorted)


def _qk(q, k):
    return lax.dot_general(q, k, (((1,), (1,)), ((), ())), preferred_element_type=F32)


def _causal_bias_t(t, copies):
    k = lax.broadcasted_iota(I32, (t, t), 0)
    q = lax.broadcasted_iota(I32, (t, t), 1)
    return jnp.concatenate([jnp.where(k <= q, 0.0, NEG)] * copies, axis=1)


def _online_softmax_step_t(m_ref, acc_ref, idx, s_t, v_t):
    m_old = m_ref[idx]
    m_new = jnp.maximum(m_old, jnp.max(s_t, axis=0, keepdims=True))
    p = jnp.exp(s_t - m_new)
    acc_ref[idx] = jnp.exp(m_old - m_new) * acc_ref[idx] + jnp.dot(v_t, p.astype(BF16), preferred_element_type=F32)
    m_ref[idx] = m_new


def _moba_prompt_kernel(qt_ref, ka_ref, vt_ref, km_ref, o_ref, q_scr, m_ref, acc_ref, *, nb):
    i = pl.program_id(1)
    blk_rows = MOBA_BLOCK
    rep = MOBA_HEADS // MOBA_KV_HEADS
    cols = rep * blk_rows
    blk = lax.broadcasted_iota(I32, (SEL_LANES, cols), 0)
    for g in range(MOBA_KV_HEADS):
        q_t = jnp.concatenate([qt_ref[0, (g * rep + j) * LANE:(g * rep + j + 1) * LANE, :] for j in range(rep)],
                              axis=1)
        gate = jnp.dot(km_ref[0, g], q_t.astype(F32), preferred_element_type=F32, precision=HIGHEST)
        valid = blk < i
        gate = jnp.where(valid, gate, -jnp.inf)
        rank = jnp.zeros((SEL_LANES, cols), I32)
        for mb in range(nb):
            gm = gate[mb:mb + 1, :]
            beats = (gm > gate) | ((gm == gate) & (blk > mb))
            rank = rank + beats.astype(I32)
        keep = (valid & (rank < MOBA_TOPK)) | (blk == i)
        bias = jnp.where((blk < nb) & jnp.logical_not(keep), NEG, 0.0)
        q_scr[g] = jnp.concatenate([q_t[0:SEL_LANE0], bias.astype(BF16), q_t[SEL_LANE0 + SEL_LANES:LANE]], axis=0)
        m_ref[g] = jnp.full((1, cols), NEG, F32)
        acc_ref[g] = jnp.zeros((MOBA_V_ROWS, cols), F32)

    def step(j, g, bias_t):
        off = pl.multiple_of(j * blk_rows, blk_rows)
        s_t = jnp.dot(ka_ref[pl.ds(off, blk_rows), g * LANE:(g + 1) * LANE], q_scr[g], preferred_element_type=F32)
        if bias_t is not None:
            s_t = s_t + bias_t
        _online_softmax_step_t(m_ref, acc_ref, g, s_t, vt_ref[j, g * LANE:g * LANE + MOBA_V_ROWS, :])

    def past(j, c):
        for g in range(MOBA_KV_HEADS):
            step(j, g, None)
        return c

    lax.fori_loop(0, i, past, 0)
    causal_t = _causal_bias_t(blk_rows, rep)
    for g in range(MOBA_KV_HEADS):
        step(i, g, causal_t)
        acc = acc_ref[g]
        o_t = acc[0:MOBA_HEAD_DIM] / acc[ONES_LANE:ONES_LANE + 1]
        o_t = jnp.concatenate([o_t, jnp.zeros_like(o_t)], axis=0)
        for j in range(rep):
            hd = g * rep + j
            o_ref[:, hd * 64:(hd + 1) * 64] = o_t[:, j * blk_rows:(j + 1) * blk_rows].T[:, 0:64].astype(BF16)


def _moba_prompt(qt, ka, vt, km, batch, seq):
    nb = seq // MOBA_BLOCK
    cols = (MOBA_HEADS // MOBA_KV_HEADS) * MOBA_BLOCK
    return pl.pallas_call(
        functools.partial(_moba_prompt_kernel, nb=nb),
        grid=(batch, nb),
        in_specs=[pl.BlockSpec((1, 1024, MOBA_BLOCK), lambda b, i: (b * nb + i, 0, 0)),
                  pl.BlockSpec((seq, 256), lambda b, i: (b, 0)),
                  pl.BlockSpec((nb, 256, MOBA_BLOCK), lambda b, i: (b, 0, 0)),
                  pl.BlockSpec((1, MOBA_KV_HEADS, SEL_LANES, LANE), lambda b, i: (b, 0, 0, 0))],
        out_specs=pl.BlockSpec((MOBA_BLOCK, 512), lambda b, i: (b * nb + i, 0)),
        out_shape=jax.ShapeDtypeStruct((batch * seq, 512), BF16),
        scratch_shapes=[pltpu.VMEM((MOBA_KV_HEADS, LANE, cols), BF16), pltpu.VMEM((MOBA_KV_HEADS, 1, cols), F32),
                        pltpu.VMEM((MOBA_KV_HEADS, MOBA_V_ROWS, cols), F32)],
        compiler_params=_cparams(("arbitrary", "arbitrary")),
        name="moba_prompt",
    )(qt, ka, vt, km)


def _mla_prompt_kernel(qt_ref, k_ref, kt_ref, o_ref, q_scr, m_ref, acc_ref):
    i = pl.program_id(1)
    tq = TOKEN_TILE
    cols = MLA_HEADS * tq
    q_scr[...] = jnp.concatenate([qt_ref[0, hd * 256:(hd + 1) * 256, :] for hd in range(MLA_HEADS)], axis=1)
    m_ref[0] = jnp.full((1, cols), NEG, F32)
    acc_ref[0] = jnp.zeros((MLA_V_ROWS, cols), F32)

    def step(j, bias_t):
        s_t = jnp.dot(k_ref[pl.ds(pl.multiple_of(j * tq, tq), tq), :], q_scr[...], preferred_element_type=F32)
        if bias_t is not None:
            s_t = s_t + bias_t
        _online_softmax_step_t(m_ref, acc_ref, 0, s_t, kt_ref[j, 0:MLA_V_ROWS, :])

    def past(j, c):
        step(j, None)
        return c

    lax.fori_loop(0, i, past, 0)
    step(i, _causal_bias_t(tq, MLA_HEADS))
    acc = acc_ref[0]
    o_t = acc[0:MLA_KV_RANK] / acc[MLA_ONES_LANE:MLA_ONES_LANE + 1]
    for hd in range(MLA_HEADS):
        o_ref[:, hd * 128:(hd + 1) * 128] = o_t[:, hd * tq:(hd + 1) * tq].T.astype(BF16)


def _mla_prompt(qft, kfull, kft, batch, seq):
    nq = seq // TOKEN_TILE
    cols = MLA_HEADS * TOKEN_TILE
    return pl.pallas_call(
        _mla_prompt_kernel,
        grid=(batch, nq),
        in_specs=[pl.BlockSpec((1, MLA_HEADS * 256, TOKEN_TILE), lambda b, i: (b * nq + i, 0, 0)),
                  pl.BlockSpec((seq, 256), lambda b, i: (b, 0)),
                  pl.BlockSpec((nq, 256, TOKEN_TILE), lambda b, i: (b, 0, 0))],
        out_specs=pl.BlockSpec((TOKEN_TILE, 1024), lambda b, i: (b * nq + i, 0)),
        out_shape=jax.ShapeDtypeStruct((batch * seq, 1024), BF16),
        scratch_shapes=[pltpu.VMEM((256, cols), BF16), pltpu.VMEM((1, 1, cols), F32),
                        pltpu.VMEM((1, MLA_V_ROWS, cols), F32)],
        compiler_params=_cparams(("arbitrary", "arbitrary")),
        name="mla_prompt",
    )(qft, kfull, kft)


def _lambda_full(lam_ref, lam_init):
    lam = lam_ref[...]
    a = jnp.sum(lam[0:1] * lam[1:2], axis=-1, keepdims=True)
    b = jnp.sum(lam[2:3] * lam[3:4], axis=-1, keepdims=True)
    return jnp.exp(a) - jnp.exp(b) + lam_init


def _diff_prompt_kernel(qt_ref, ka_ref, vt_ref, lam_ref, subg_ref, o_ref, q_scr, m_ref, acc_ref, *, lam_init):
    i = pl.program_id(1)
    tq = TOKEN_TILE
    rep = DIFF_HEADS // DIFF_KV_HEADS
    cols = rep * tq
    chains = [(g, c) for g in range(DIFF_KV_HEADS) for c in range(2)]
    for n, (g, c) in enumerate(chains):
        q_scr[n] = jnp.concatenate(
            [qt_ref[0, ((g * rep + j) * 2 + c) * LANE:((g * rep + j) * 2 + c + 1) * LANE, :] for j in range(rep)],
            axis=1)
        m_ref[n] = jnp.full((1, cols), NEG, F32)
        acc_ref[n] = jnp.zeros((DIFF_V_ROWS, cols), F32)

    def step(j, n, bias_t):
        g, c = chains[n]
        off = pl.multiple_of(j * tq, tq)
        s_t = jnp.dot(ka_ref[pl.ds(off, tq), (g * 2 + c) * LANE:(g * 2 + c + 1) * LANE], q_scr[n],
                      preferred_element_type=F32)
        if bias_t is not None:
            s_t = s_t + bias_t
        _online_softmax_step_t(m_ref, acc_ref, n, s_t, vt_ref[j, g * 256:g * 256 + DIFF_V_ROWS, :])

    def past(j, cc):
        for n in range(len(chains)):
            step(j, n, None)
        return cc

    lax.fori_loop(0, i, past, 0)
    causal_t = _causal_bias_t(tq, rep)
    lam_full = _lambda_full(lam_ref, lam_init)
    outs = []
    for n in range(len(chains)):
        step(i, n, causal_t)
        acc = acc_ref[n]
        outs.append(acc[0:128] / acc[128:129])
    for g in range(DIFF_KV_HEADS):
        o_t = outs[2 * g] - lam_full * outs[2 * g + 1]
        o_t = o_t * lax.rsqrt(jnp.mean(o_t * o_t, axis=0, keepdims=True) + EPS)
        for j in range(rep):
            hd = g * rep + j
            o = o_t[:, j * tq:(j + 1) * tq].T * subg_ref[...] * (1.0 - lam_init)
            o_ref[:, hd * 128:(hd + 1) * 128] = o.astype(BF16)


def _diff_prompt(qt, ka, vt, lam, subg, lam_init, batch, seq):
    nq = seq // TOKEN_TILE
    cols = (DIFF_HEADS // DIFF_KV_HEADS) * TOKEN_TILE
    n_chain = DIFF_KV_HEADS * 2
    return pl.pallas_call(
        functools.partial(_diff_prompt_kernel, lam_init=lam_init),
        grid=(batch, nq),
        in_specs=[pl.BlockSpec((1, 2048, TOKEN_TILE), lambda b, i: (b * nq + i, 0, 0)),
                  pl.BlockSpec((seq, 512), lambda b, i: (b, 0)),
                  pl.BlockSpec((nq, 512, TOKEN_TILE), lambda b, i: (b, 0, 0)),
                  pl.BlockSpec((4, 64), lambda b, i: (0, 0)),
                  pl.BlockSpec((1, 128), lambda b, i: (0, 0))],
        out_specs=pl.BlockSpec((TOKEN_TILE, 1024), lambda b, i: (b * nq + i, 0)),
        out_shape=jax.ShapeDtypeStruct((batch * seq, 1024), BF16),
        scratch_shapes=[pltpu.VMEM((n_chain, LANE, cols), BF16), pltpu.VMEM((n_chain, 1, cols), F32),
                        pltpu.VMEM((n_chain, DIFF_V_ROWS, cols), F32)],
        compiler_params=_cparams(("arbitrary", "arbitrary")),
        name="diff_prompt",
    )(qt, ka, vt, lam, subg)


def _page_copy(src_hbm, page, dst, sem):
    return pltpu.make_async_copy(src_hbm.at[page], dst, sem)


def _paged_pipeline(pt_ref, n_pages, layer_off, streams, sem, small_stream=None):
    s = pl.program_id(0)
    ns = pl.num_programs(0)
    slot = s % 2

    def start(seq, slot_):
        for p in range(n_pages):
            page = pt_ref[seq * n_pages + p] + layer_off
            for n, (hbm, buf, dst_fn) in enumerate(streams):
                _page_copy(hbm, page, dst_fn(buf, slot_, p), sem.at[slot_]).start(priority=int(n == small_stream))

    @pl.when(s == 0)
    def _():
        start(0, 0)

    @pl.when(s + 1 < ns)
    def _():
        start(s + 1, 1 - slot)

    for p in range(n_pages):
        for hbm, buf, dst_fn in streams:
            _page_copy(hbm, 0, dst_fn(buf, slot, p), sem.at[slot]).wait()
    return slot


def _lane_window(buf, slot, p):
    return buf.at[slot, :, pl.ds(p * LANE, LANE)]


def _row_window(rows):
    return lambda buf, slot, p: buf.at[slot, pl.ds(p * rows, rows), :]


def _moba_decode_kernel(pt_ref, q_ref, knew_ref, vnew_ref, k_hbm, v_hbm, o_ref, kbuf, vbuf, sem, *,
                        n_pages, layer_off, past_len):
    slot = _paged_pipeline(pt_ref, n_pages, layer_off, [(k_hbm, kbuf, _lane_window), (v_hbm, vbuf, _lane_window)],
                           sem)
    nblk = past_len // MOBA_BLOCK
    q = q_ref[0]
    kt = kbuf[slot].astype(BF16)
    s_raw = jnp.dot(q, kt, preferred_element_type=F32)
    lane = lax.broadcasted_iota(I32, (MOBA_HEADS, LANE), 1)
    gate = jnp.full((MOBA_HEADS, LANE), -jnp.inf, F32)
    for b in range(nblk):
        gs = jnp.sum(s_raw[:, b * MOBA_BLOCK:(b + 1) * MOBA_BLOCK], axis=-1, keepdims=True)
        gate = jnp.where(lane == b, gs, gate)
    rank = jnp.zeros((MOBA_HEADS, LANE), I32)
    for mb in range(nblk):
        gm = gate[:, mb:mb + 1]
        beats = (gm > gate) | ((gm == gate) & (lane > mb))
        rank = rank + beats.astype(I32)
    sel = (lane < nblk) & (rank < MOBA_TOPK)
    selmask = jnp.concatenate(
        [jnp.broadcast_to(jnp.sum(jnp.where((lane == b) & sel, 1.0, 0.0), axis=-1, keepdims=True) > 0.5,
                          (MOBA_HEADS, MOBA_BLOCK)) for b in range(nblk)], axis=1)
    slope = _slope_column(_alibi_slopes(MOBA_HEADS))
    pos = lax.broadcasted_iota(I32, (MOBA_HEADS, past_len), 1)
    dist = (past_len - pos).astype(F32)
    s = jnp.where(selmask, s_raw - slope * dist, NEG)
    s_own = jnp.sum(q.astype(F32) * knew_ref[0], axis=-1, keepdims=True)
    m = jnp.maximum(jnp.max(s, axis=-1, keepdims=True), s_own)
    p = jnp.exp(s - m)
    p_own = jnp.exp(s_own - m)
    l = jnp.sum(p, axis=-1, keepdims=True) + p_own
    vt = vbuf[slot].astype(BF16)
    acc = _qk(p.astype(BF16), vt) + p_own * vnew_ref[0]
    o_ref[0] = acc / l


def _moba_decode(pt_flat, q_bd, knew, vnew, k_pages, v_pages, layer, n_pool, n_seq, n_pages):
    past_len = n_pages * LANE
    kern = functools.partial(_moba_decode_kernel, n_pages=n_pages, layer_off=layer * n_pool, past_len=past_len)
    return pl.pallas_call(
        kern,
        grid_spec=pltpu.PrefetchScalarGridSpec(
            num_scalar_prefetch=1,
            grid=(n_seq,),
            in_specs=[pl.BlockSpec((1, 8, 128), lambda s, pt: (s, 0, 0)),
                      pl.BlockSpec((1, 1, 128), lambda s, pt: (s, 0, 0)),
                      pl.BlockSpec((1, 1, 128), lambda s, pt: (s, 0, 0)),
                      pl.BlockSpec(memory_space=pl.ANY), pl.BlockSpec(memory_space=pl.ANY)],
            out_specs=pl.BlockSpec((1, 8, 128), lambda s, pt: (s, 0, 0)),
            scratch_shapes=[pltpu.VMEM((2, 128, past_len), F32), pltpu.VMEM((2, 128, past_len), F32),
                            pltpu.SemaphoreType.DMA((2,))]),
        out_shape=jax.ShapeDtypeStruct((n_seq, 8, 128), F32),
        compiler_params=_cparams(("arbitrary",)),
        name="moba_decode",
    )(pt_flat, q_bd, knew, vnew, k_pages, v_pages)


def _mla_decode_kernel(pt_ref, qlat_ref, qpe_ref, cnew_ref, pnew_ref, c_hbm, p_hbm, o_ref, cbuf, pbuf, sem, *,
                       n_pages, layer_off):
    slot = _paged_pipeline(pt_ref, n_pages, layer_off,
                           [(c_hbm, cbuf, _row_window(LANE)), (p_hbm, pbuf, _lane_window)], sem, small_stream=1)
    qlat = qlat_ref[0]
    qpe = qpe_ref[0]
    ckv = cbuf[slot].astype(BF16)
    kpet = pbuf[slot].astype(BF16)
    s = _qk(qlat, ckv) + jnp.dot(qpe, kpet, preferred_element_type=F32)
    s_own = (jnp.sum(qlat.astype(F32) * cnew_ref[0], axis=-1, keepdims=True)
             + jnp.sum(qpe.astype(F32) * pnew_ref[0], axis=-1, keepdims=True))
    m = jnp.maximum(jnp.max(s, axis=-1, keepdims=True), s_own)
    p = jnp.exp(s - m)
    p_own = jnp.exp(s_own - m)
    l = jnp.sum(p, axis=-1, keepdims=True) + p_own
    acc = jnp.dot(p.astype(BF16), ckv, preferred_element_type=F32) + p_own * cnew_ref[0]
    o_ref[0] = (acc / l).astype(BF16)


def _mla_decode(pt_flat, qlat, qpe, cnew, pnew, c_pages, p_pages, layer, n_pool, n_seq, n_pages):
    past_len = n_pages * LANE
    kern = functools.partial(_mla_decode_kernel, n_pages=n_pages, layer_off=layer * n_pool)
    return pl.pallas_call(
        kern,
        grid_spec=pltpu.PrefetchScalarGridSpec(
            num_scalar_prefetch=1,
            grid=(n_seq,),
            in_specs=[pl.BlockSpec((1, 8, 128), lambda s, pt: (s, 0, 0)),
                      pl.BlockSpec((1, 8, 32), lambda s, pt: (s, 0, 0)),
                      pl.BlockSpec((1, 1, 128), lambda s, pt: (s, 0, 0)),
                      pl.BlockSpec((1, 1, 32), lambda s, pt: (s, 0, 0)),
                      pl.BlockSpec(memory_space=pl.ANY), pl.BlockSpec(memory_space=pl.ANY)],
            out_specs=pl.BlockSpec((1, 8, 128), lambda s, pt: (s, 0, 0)),
            scratch_shapes=[pltpu.VMEM((2, past_len, 128), F32), pltpu.VMEM((2, MLA_ROPE_DIM, past_len), F32),
                            pltpu.SemaphoreType.DMA((2,))]),
        out_shape=jax.ShapeDtypeStruct((n_seq, 8, 128), BF16),
        compiler_params=_cparams(("arbitrary",)),
        name="mla_decode",
    )(pt_flat, qlat, qpe, cnew, pnew, c_pages, p_pages)


def _diff_decode_kernel(pt_ref, q_ref, knew_ref, vnew_ref, lam_ref, subg_ref, k_hbm, v_hbm, o_ref,
                        kbuf, vbuf, sem, m_ref, l_ref, acc_ref, *, n_pages, pages_per_step, layer_off, lam_init):
    st = pl.program_id(0)
    halves = n_pages // pages_per_step
    half = st % halves
    slot = _paged_pipeline(pt_ref, pages_per_step, layer_off,
                           [(k_hbm, kbuf, _lane_window), (v_hbm, vbuf, _row_window(2 * LANE))], sem)
    chunk = pages_per_step * LANE
    q = q_ref[0]
    kt = kbuf[slot].astype(BF16)
    head_slopes = _alibi_slopes(DIFF_HEADS)
    slope = _slope_column([head_slopes[(r // 8) * 4 + (r % 4)] for r in range(16)])
    pos = lax.broadcasted_iota(I32, (16, chunk), 1) + half * chunk
    dist = (n_pages * LANE - pos).astype(F32)
    s = jnp.dot(q, kt, preferred_element_type=F32) - slope * dist

    @pl.when(half == 0)
    def _():
        m_ref[...] = jnp.full_like(m_ref, NEG)
        l_ref[...] = jnp.zeros_like(l_ref)
        acc_ref[...] = jnp.zeros_like(acc_ref)

    m_old = m_ref[...]
    m_new = jnp.maximum(m_old, jnp.max(s, axis=-1, keepdims=True))
    alpha = jnp.exp(m_old - m_new)
    p = jnp.exp(s - m_new)
    l_ref[...] = alpha * l_ref[...] + jnp.sum(p, axis=-1, keepdims=True)
    pb = p.astype(BF16)
    pv = []
    for g in range(DIFF_KV_HEADS):
        vg = vbuf[slot, pl.ds(g, chunk, stride=2), :].astype(BF16)
        pv.append(jnp.dot(pb[g * 8:(g + 1) * 8], vg, preferred_element_type=F32))
    acc_ref[...] = alpha * acc_ref[...] + jnp.concatenate(pv, axis=0)
    m_ref[...] = m_new

    @pl.when(half == halves - 1)
    def _():
        s_own = jnp.sum(q.astype(F32) * knew_ref[0], axis=-1, keepdims=True)
        m_o = m_ref[...]
        m_f = jnp.maximum(m_o, s_own)
        a = jnp.exp(m_o - m_f)
        p_own = jnp.exp(s_own - m_f)
        l = a * l_ref[...] + p_own
        vnew = vnew_ref[0]
        vrows = jnp.concatenate([jnp.broadcast_to(vnew[:, g * 128:(g + 1) * 128], (8, 128))
                                 for g in range(DIFF_KV_HEADS)], axis=0)
        o = (a * acc_ref[...] + p_own * vrows) / l
        lam_full = _lambda_full(lam_ref, lam_init)
        res = []
        for g in range(DIFF_KV_HEADS):
            o1 = o[g * 8:g * 8 + 4]
            o2 = o[g * 8 + 4:g * 8 + 8]
            res.append(_rms(o1 - lam_full * o2, subg_ref[...]) * (1.0 - lam_init))
        o_ref[0] = jnp.concatenate(res, axis=0).astype(BF16)


def _diff_decode(pt_flat, q_bd, knew, vnew, lam, subg, k_pages, v_pages, layer, n_pool, n_seq, n_pages, lam_init):
    halves = 2
    pps = n_pages // halves
    chunk = pps * LANE
    kern = functools.partial(_diff_decode_kernel, n_pages=n_pages, pages_per_step=pps, layer_off=layer * n_pool,
                             lam_init=lam_init)
    return pl.pallas_call(
        kern,
        grid_spec=pltpu.PrefetchScalarGridSpec(
            num_scalar_prefetch=1,
            grid=(n_seq * halves,),
            in_specs=[pl.BlockSpec((1, 16, 256), lambda s, pt: (s // halves, 0, 0)),
                      pl.BlockSpec((1, 1, 256), lambda s, pt: (s // halves, 0, 0)),
                      pl.BlockSpec((1, 1, 256), lambda s, pt: (s // halves, 0, 0)),
                      pl.BlockSpec((4, 64), lambda s, pt: (0, 0)),
                      pl.BlockSpec((1, 128), lambda s, pt: (0, 0)),
                      pl.BlockSpec(memory_space=pl.ANY), pl.BlockSpec(memory_space=pl.ANY)],
            out_specs=pl.BlockSpec((1, 8, 128), lambda s, pt: (s // halves, 0, 0)),
            scratch_shapes=[pltpu.VMEM((2, 256, chunk), F32), pltpu.VMEM((2, 2 * chunk, 128), F32),
                            pltpu.SemaphoreType.DMA((2,)),
                            pltpu.VMEM((16, 1), F32), pltpu.VMEM((16, 1), F32), pltpu.VMEM((16, 128), F32)]),
        out_shape=jax.ShapeDtypeStruct((n_seq, 8, 128), BF16),
        compiler_params=_cparams(("arbitrary",)),
        name="diff_decode",
    )(pt_flat, q_bd, knew, vnew, lam, subg, k_pages, v_pages)


def _rope_tables(pos):
    half = MLA_ROPE_DIM // 2
    inv = ROPE_BASE ** (-jnp.arange(half, dtype=F32) / half)
    ang = pos.astype(F32)[:, None] * inv[None, :]
    pad = jnp.zeros((pos.shape[0], LANE - MLA_ROPE_DIM), F32)
    cos, sin = jnp.cos(ang), jnp.sin(ang)
    return jnp.concatenate([cos, cos, pad], axis=1), jnp.concatenate([sin, sin, pad], axis=1)


def _swap_halves(w):
    half = w.shape[-1] // 2
    return jnp.concatenate([-w[..., half:], w[..., :half]], axis=-1)


def _pad_cols(w, width):
    return jnp.pad(w, [(0, 0)] * (w.ndim - 1) + [(0, width - w.shape[-1])])


def _pad_blocks(w, block, width):
    rows = w.shape[0]
    return _pad_cols(w.reshape(rows, -1, block), width).reshape(rows, -1)


def _ab_weights(w_in, w_uq, w_uk, w_uv):
    wq, wk, wv = w_in[:, 0:512], w_in[:, 512:640], w_in[:, 640:768]
    kpe_w = w_in[:, 1152:1184]
    w1 = jnp.concatenate([_pad_blocks(wq, 64, LANE), wk, wv, _pad_blocks(wk, 64, LANE), _pad_blocks(wv, 64, LANE),
                          w_in[:, 768:1152], _pad_cols(kpe_w, LANE), _pad_cols(_swap_halves(kpe_w), LANE)], axis=1)
    uq = w_uq.reshape(MLA_Q_RANK, MLA_HEADS, MLA_NOPE_DIM + MLA_ROPE_DIM)
    nope = uq[:, :, :MLA_NOPE_DIM].reshape(MLA_Q_RANK, MLA_HEADS * MLA_NOPE_DIM)
    pe = uq[:, :, MLA_NOPE_DIM:]
    pe_pad = _pad_cols(pe, LANE).reshape(MLA_Q_RANK, MLA_HEADS * LANE)
    pes_pad = _pad_cols(_swap_halves(pe), LANE).reshape(MLA_Q_RANK, MLA_HEADS * LANE)
    wuq = jnp.concatenate([nope, pe_pad, pes_pad], axis=1)
    wukt = jnp.transpose(w_uk, (1, 2, 0))
    eye = jnp.eye(MLA_HEADS, dtype=F32)
    wuv_bd = (jnp.transpose(w_uv, (1, 0, 2))[:, :, None, :] * eye[:, None, :, None]).reshape(
        MLA_HEADS * MLA_KV_RANK, MLA_HEADS * MLA_V_DIM)
    return w1.astype(BF16), wuq.astype(BF16), wukt.astype(BF16), wuv_bd.astype(BF16)


def _c_weights(w_in):
    wq, wk, wv = w_in[:, 0:1024], w_in[:, 1024:1280], w_in[:, 1280:1536]
    w = jnp.concatenate([_pad_blocks(wq, 64, LANE), wk, wv, _pad_blocks(wk, 64, LANE), _pad_blocks(wv, 128, 256)],
                        axis=1)
    return w.astype(BF16)


def _lane_table(rows, width, entries):
    lane = jnp.arange(width, dtype=I32)[None, :]
    tab = jnp.zeros((rows, width), F32)
    for ln, val in entries.items():
        tab = jnp.where(lane == ln, jnp.broadcast_to(jnp.asarray(val, F32).reshape(-1, 1), (rows, 1)), tab)
    return tab


def _query_consts(slopes, blocks_per_head):
    ent = {}
    for h, s in enumerate(slopes):
        for c in range(blocks_per_head):
            base = (h * blocks_per_head + c) * LANE
            ent[base + POS_HI_LANE] = s * 256.0
            ent[base + POS_LO_LANE] = s
    return _lane_table(1, len(slopes) * blocks_per_head * LANE, ent)


def _key_tables(seq):
    pos = jnp.arange(seq, dtype=I32)
    hi = (pos // 256).astype(F32)
    lo = (pos % 256).astype(F32)
    nb = seq // MOBA_BLOCK
    ent = {}
    for g in range(MOBA_KV_HEADS):
        ent[g * LANE + POS_HI_LANE] = hi
        ent[g * LANE + POS_LO_LANE] = lo
        for n in range(nb):
            ent[g * LANE + SEL_LANE0 + n] = (pos // MOBA_BLOCK == n).astype(F32)
        ent[(MOBA_KV_HEADS + g) * LANE + ONES_LANE] = jnp.ones((seq,), F32)
    ktab_ab = _lane_table(seq, 2 * MOBA_KV_HEADS * LANE, ent)
    ent = {}
    for gc in range(2 * DIFF_KV_HEADS):
        ent[gc * LANE + POS_HI_LANE] = hi
        ent[gc * LANE + POS_LO_LANE] = lo
    for g in range(DIFF_KV_HEADS):
        ent[512 + g * 256 + 128] = jnp.ones((seq,), F32)
    ktab_c = _lane_table(seq, 1024, ent)
    return ktab_ab, ktab_c


def _block_diag_rows(q, blocks, width):
    n, rows, _ = q.shape
    nblk = max(blocks) + 1
    sel = jnp.asarray([[1.0 if blocks[r] == b else 0.0 for b in range(nblk)] for r in range(rows)], q.dtype)
    return (q[:, :, None, :] * sel[None, :, :, None]).reshape(n, rows, nblk * width)


def kernel(x_prompt, x_sample, c_prompt, c_sample, cache_moba_k, cache_moba_v, cache_mla_ckv, cache_mla_kpe,
           cache_diff_k, cache_diff_v, page_table, mod_w, mod_b, norm_mix, norm_ffn, ab_w_in, mla_q_norm,
           mla_kv_norm, mla_w_uq, mla_w_uk, mla_w_uv, ab_w_out, c_w_in, diff_lambda, diff_subln, c_w_out,
           moe_w_group, moe_b_group, moe_w_expert, moe_b_expert, moe_w1, moe_w3, moe_w2, final_norm):
    batch, seq, d = x_prompt.shape
    n_seq = x_sample.shape[0]
    depth = mod_w.shape[0]
    n_pool = cache_moba_k.shape[1]
    n_pages = page_table.shape[1]
    past_len = n_pages * cache_moba_k.shape[2]
    nb = seq // MOBA_BLOCK
    assert d == D_MODEL and x_sample.shape[1] == 1 and cache_moba_k.shape[2] == LANE
    assert seq % MOBA_BLOCK == 0 and past_len % MOBA_BLOCK == 0 and n_seq % 8 == 0
    assert nb <= SEL_LANES and seq <= 256 * 256
    n_p = batch * seq
    tpb = seq // TOKEN_TILE
    wide = math.gcd(seq, WIDE_TILE)
    wpb = seq // wide

    xp = x_prompt.reshape(n_p, d)
    xs = x_sample.reshape(n_seq, d)
    mod = _modulation(jnp.concatenate([c_prompt, c_sample], axis=0), mod_w, mod_b)
    mod = mod.reshape(depth, batch + n_seq, 6, d)
    pt_flat = page_table.reshape(-1)

    n_all = cache_moba_k.shape[0] * n_pool
    mk_pages = jnp.transpose(cache_moba_k, (0, 1, 3, 4, 2)).reshape(n_all, 128, LANE)
    mv_pages = jnp.transpose(cache_moba_v, (0, 1, 3, 4, 2)).reshape(n_all, 128, LANE)
    ckv_pages = cache_mla_ckv.reshape(n_all, LANE, 128)
    kpe_pages = jnp.transpose(cache_mla_kpe, (0, 1, 3, 2)).reshape(n_all, MLA_ROPE_DIM, LANE)
    n_all_c = cache_diff_k.shape[0] * n_pool
    dk_pages = jnp.transpose(cache_diff_k, (0, 1, 3, 4, 5, 2)).reshape(n_all_c, 256, LANE)
    dv_pages = cache_diff_v.reshape(n_all_c, 2 * LANE, 128)

    cos_p, sin_p = _rope_tables(jnp.arange(seq, dtype=I32))
    cos_s, sin_s = _rope_tables(jnp.full((1,), past_len, I32))
    ktab_ab, ktab_c = _key_tables(seq)
    ktab_ab_s = jnp.zeros((1, ktab_ab.shape[1]), F32)
    ktab_c_s = jnp.zeros((1, ktab_c.shape[1]), F32)
    qconst_ab = _query_consts(_alibi_slopes(MOBA_HEADS), 1)
    qconst_c = _query_consts(_alibi_slopes(DIFF_HEADS), 2)
    kones = _lane_table(1, LANE, {MLA_ONES_LANE - MLA_KV_RANK: 1.0})

    n_exp_all = depth * N_EXPERTS
    w1_all = moe_w1.reshape(n_exp_all, d, EXPERT_FF)
    w3_all = moe_w3.reshape(n_exp_all, d, EXPERT_FF)
    w2_all = moe_w2.reshape(n_exp_all, EXPERT_FF, d)
    n_slots = TOP_EXPERTS * (n_p + n_seq)
    n_tiles_max = -(-n_slots // EXPERT_TILE) + N_EXPERTS
    x_sorted = jnp.zeros((n_tiles_max * EXPERT_TILE, D_MODEL), F32)
    fg = final_norm.reshape(1, d)

    moba_blocks = [h // (MOBA_HEADS // MOBA_KV_HEADS) for h in range(MOBA_HEADS)]
    diff_rows = [(g, c, j) for g in range(DIFF_KV_HEADS) for c in range(2) for j in range(4)]

    new_p = {k: [] for k in ("mk", "mv", "ckv", "kpe", "dk", "dv")}
    new_s = {k: [] for k in ("mk", "mv", "ckv", "kpe", "dk", "dv")}
    for l in range(depth):
        i = l // 2
        mp = [mod[l, :batch, k].reshape(batch, 1, d) for k in range(6)]
        ms = [mod[l, batch:, k] for k in range(6)]
        g_mix = norm_mix[l].reshape(1, d)
        if l % 2 == 0:
            w1, wuq, wukt, wuv_bd = _ab_weights(ab_w_in[i], mla_w_uq[i], mla_w_uk[i], mla_w_uv[i])
            wout = ab_w_out[i].astype(BF16)
            qg = mla_q_norm[i].reshape(1, -1)
            kvg = mla_kv_norm[i].reshape(1, -1)
            qt, mk, mv, ka, vt, kmean, qft, ckv, kpe, kfull, kft = _inproj_ab(
                xp, mp[0], mp[1], g_mix, w1, qg, kvg, wuq, wukt, cos_p, sin_p, qconst_ab, ktab_ab, kones,
                False, TOKEN_TILE, tpb)
            km = kmean.reshape(batch, tpb, MOBA_KV_HEADS, 64).transpose(0, 2, 1, 3)
            km = jnp.pad(km, ((0, 0), (0, 0), (0, SEL_LANES - tpb), (0, LANE - 64)))
            oa = _moba_prompt(qt, ka, vt, km, batch, seq)
            olat = _mla_prompt(qft, kfull, kft, batch, seq)
            xp = _outproj_ab(xp, mp[2], oa, olat, wuv_bd, wout, False, wide, wpb)
            new_p["mk"].append(mk); new_p["mv"].append(mv); new_p["ckv"].append(ckv); new_p["kpe"].append(kpe)
            qt, mk, mv, ka, vt, kmean, qft, ckv, kpe, kfull, kft = _inproj_ab(
                xs, ms[0], ms[1], g_mix, w1, qg, kvg, wuq, wukt, cos_s, sin_s, qconst_ab, ktab_ab_s, kones,
                True, n_seq, 1)
            q_heads = jnp.transpose(qt[0].reshape(MOBA_HEADS, LANE, n_seq), (2, 0, 1))
            q_bd = _block_diag_rows(q_heads[:, :, :64], moba_blocks, 64)
            o_raw = _moba_decode(pt_flat, q_bd, mk.reshape(n_seq, 1, 128), mv.reshape(n_seq, 1, 128),
                                 mk_pages, mv_pages, i, n_pool, n_seq, n_pages)
            o_raw = o_raw.reshape(n_seq, MOBA_KV_HEADS, 4, MOBA_KV_HEADS, 64)
            oa = jnp.concatenate([o_raw[:, g, :, g, :] for g in range(MOBA_KV_HEADS)], axis=1)
            oa = oa.reshape(n_seq, 512).astype(BF16)
            qf3 = jnp.transpose(qft[0].reshape(MLA_HEADS, 256, n_seq), (2, 0, 1))
            olat = _mla_decode(pt_flat, qf3[:, :, :128], qf3[:, :, 128:128 + MLA_ROPE_DIM],
                               ckv.reshape(n_seq, 1, 128), kpe.reshape(n_seq, 1, MLA_ROPE_DIM),
                               ckv_pages, kpe_pages, i, n_pool, n_seq, n_pages)
            xs = _outproj_ab(xs, ms[2], oa, olat.reshape(n_seq, 1024), wuv_bd, wout, True, n_seq, 1)
            new_s["mk"].append(mk); new_s["mv"].append(mv); new_s["ckv"].append(ckv); new_s["kpe"].append(kpe)
        else:
            lam_init = 0.8 - 0.6 * math.exp(-0.3 * l)
            wc = _c_weights(c_w_in[i])
            wout = c_w_out[i].astype(BF16)
            lam = diff_lambda[i]
            subg = diff_subln[i].reshape(1, -1)
            qt, k, v, ka, vt = _inproj_c(xp, mp[0], mp[1], g_mix, wc, qconst_c, ktab_c, False, TOKEN_TILE, tpb)
            o = _diff_prompt(qt, ka, vt, lam, subg, lam_init, batch, seq)
            xp = _outproj_c(xp, mp[2], o, wout, False, wide, wpb)
            new_p["dk"].append(k); new_p["dv"].append(v)
            qt, k, v, ka, vt = _inproj_c(xs, ms[0], ms[1], g_mix, wc, qconst_c, ktab_c_s, True, n_seq, 1)
            q4 = jnp.transpose(qt[0].reshape(DIFF_HEADS, 2, LANE, n_seq), (3, 0, 1, 2))[:, :, :, :64]
            q_rows = jnp.stack([q4[:, g * 4 + j, c, :] for (g, c, j) in diff_rows], axis=1)
            q_bd = _block_diag_rows(q_rows, [g * 2 + c for (g, c, j) in diff_rows], 64)
            o = _diff_decode(pt_flat, q_bd, k.reshape(n_seq, 1, 256), v.reshape(n_seq, 1, 256), lam, subg,
                             dk_pages, dv_pages, i, n_pool, n_seq, n_pages, lam_init)
            xs = _outproj_c(xs, ms[2], o.reshape(n_seq, 1024), wout, True, n_seq, 1)
            new_s["dk"].append(k); new_s["dv"].append(v)
        wr = jnp.concatenate([moe_w_expert[l], moe_w_group[l],
                              jnp.zeros((d, LANE - N_EXPERTS - N_GROUPS), F32)], axis=1)
        br = jnp.concatenate([moe_b_expert[l], moe_b_group[l],
                              jnp.zeros((LANE - N_EXPERTS - N_GROUPS,), F32)]).reshape(1, LANE)
        wr_hi = wr.astype(BF16)
        wr2 = jnp.stack([wr_hi, (wr - wr_hi.astype(F32)).astype(BF16)])
        g_ffn = norm_ffn[l].reshape(1, d)
        hp_p, info_p = _router(xp, mp[3], mp[4], g_ffn, wr2, br, False, wide, wpb)
        hp_s, info_s = _router(xs, ms[3], ms[4], g_ffn, wr2, br, True, n_seq, 1)
        ids = jnp.concatenate([info_p[:, 0:TOP_EXPERTS], info_s[:, 0:TOP_EXPERTS]], axis=0).astype(I32).reshape(-1)
        dest, tile_expert, n_tiles = _routing_offsets(ids, n_tiles_max)
        dest_p = dest[:TOP_EXPERTS * n_p].reshape(n_p // TOKEN_TILE, 1, TOP_EXPERTS * TOKEN_TILE)
        dest_s = dest[TOP_EXPERTS * n_p:].reshape(1, 1, TOP_EXPERTS * n_seq)
        x_sorted = _dispatch(dest_p, hp_p, x_sorted, TOKEN_TILE)
        x_sorted = _dispatch(dest_s, hp_s, x_sorted, n_seq)
        y_sorted = _experts(tile_expert + l * N_EXPERTS, n_tiles, x_sorted, w1_all, w3_all, w2_all)
        last = l == depth - 1
        xp = _combine(dest_p, info_p, xp, mp[5], fg, y_sorted, False, TOKEN_TILE, tpb, last)
        xs = _combine(dest_s, info_s, xs, ms[5], fg, y_sorted, True, n_seq, 1, last)

    y_prompt = xp.reshape(batch, seq, d)
    y_sample = xs.reshape(n_seq, 1, d)

    def stack(lst, shape):
        return jnp.stack([a.reshape(shape) for a in lst])

    outs_p = (stack(new_p["mk"], (batch, seq, MOBA_KV_HEADS, MOBA_HEAD_DIM)),
              stack(new_p["mv"], (batch, seq, MOBA_KV_HEADS, MOBA_HEAD_DIM)),
              stack(new_p["ckv"], (batch, seq, MLA_KV_RANK)),
              stack(new_p["kpe"], (batch, seq, MLA_ROPE_DIM)),
              stack(new_p["dk"], (batch, seq, DIFF_KV_HEADS, 2, DIFF_HEAD_DIM)),
              stack(new_p["dv"], (batch, seq, DIFF_KV_HEADS, 2 * DIFF_HEAD_DIM)))
    outs_s = (stack(new_s["mk"], (n_seq, 1, MOBA_KV_HEADS, MOBA_HEAD_DIM)),
              stack(new_s["mv"], (n_seq, 1, MOBA_KV_HEADS, MOBA_HEAD_DIM)),
              stack(new_s["ckv"], (n_seq, 1, MLA_KV_RANK)),
              stack(new_s["kpe"], (n_seq, 1, MLA_ROPE_DIM)),
              stack(new_s["dk"], (n_seq, 1, DIFF_KV_HEADS, 2, DIFF_HEAD_DIM)),
              stack(new_s["dv"], (n_seq, 1, DIFF_KV_HEADS, 2 * DIFF_HEAD_DIM)))
    return (y_prompt, y_sample) + outs_p + outs_s
```
